```python
import jax, jax.numpy as jnp
from jax import lax
import numpy as np

D_MODEL = 1024
BATCH = 8
SEQ = 2048
DEPTH = 2
DEC_BATCH = 128
DEC_SEQ = 8
PAST_LEN = 16384
PAGE_SIZE = 128

N_META = 16
BLOCK = 128
PAD = BLOCK - N_META
D_A = D_MODEL // 2
POOL_WINDOWS = (2, 4, 8, 16)
N_POOL_GROUPS = 4
GC = D_A // N_POOL_GROUPS
W_MAX = 16
POOL_BUF = W_MAX - 1
H_B = D_MODEL // 128
DH_B = 64
D_B = H_B * DH_B
FORGET_BIAS_LO = 2.0
FORGET_BIAS_HI = 12.0
H_C = D_MODEL // 256
DK_C = 128
DV_C = 128
HK_C = H_C * DK_C
D_C = H_C * DV_C
HGRN_CHUNK = 64
H_D = D_MODEL // 128
KV_D = H_D // 4
G_D = H_D // KV_D
DH_D = 64
D_D = H_D * DH_D
WINDOW = 128
N_EVEN = (DEPTH + 1) // 2
N_ODD = DEPTH // 2
ALPHA = (2 * DEPTH) ** 0.25
BETA = (8 * DEPTH) ** -0.25
LN_EPS = 1e-5
NEG = -1e30
E_SPLIT = (D_A, D_A, D_B, D_B, D_B, D_B, H_B)
O_SPLIT = (HK_C, HK_C, D_C, D_C, D_D, KV_D * DH_D, KV_D * DH_D, D_D)
IN_E = sum(E_SPLIT)
IN_O = sum(O_SPLIT)

kernel_name = 'hybrid_pool_fox_hgrn2_swa_decode_step'

F32 = jnp.float32


def _split(z, sizes):
    return jnp.split(z, np.cumsum(sizes)[:-1].tolist(), axis=-1)


def _layer_norm(x, g, b):
    xf = x.astype(F32)
    mu = xf.mean(-1, keepdims=True)
    var = jnp.mean(jnp.square(xf - mu), -1, keepdims=True)
    return ((xf - mu) * lax.rsqrt(var + LN_EPS) * g.astype(F32) + b.astype(F32)).astype(x.dtype)


def _alibi_slopes():
    return jnp.exp2(-8.0 * jnp.arange(1, H_D + 1, dtype=F32) / H_D)


def _pool_mix(u, valid, w_pool, scale):
    n, L = u.shape[:2]
    vf = valid.astype(F32)
    uf = u.astype(F32) * vf[..., None]
    cs = jnp.cumsum(jnp.pad(uf, ((0, 0), (W_MAX, 0), (0, 0))), axis=1)
    cn = jnp.cumsum(jnp.pad(vf, ((0, 0), (W_MAX, 0))), axis=1)
    means = []
    for g, w in enumerate(POOL_WINDOWS):
        ch = slice(g * GC, (g + 1) * GC)
        s = cs[:, W_MAX:, ch] - cs[:, W_MAX - w:W_MAX - w + L, ch]
        c = cn[:, W_MAX:] - cn[:, W_MAX - w:W_MAX - w + L]
        means.append(s / jnp.maximum(c, 1.0)[..., None])
    d = (jnp.concatenate(means, -1) - uf).reshape(n, L, N_POOL_GROUPS, GC)
    y = jnp.einsum('nlgc,gce->nlge', d, w_pool.astype(F32)).reshape(n, L, D_A)
    return (y * scale.astype(F32)).astype(u.dtype)


def _fox_prompt(q, k, v, logf, valid):
    b, L, h, dh = q.shape
    nb = L // BLOCK
    F = jnp.cumsum(logf, axis=1)
    Ft = F.transpose(0, 2, 1)
    pos = jnp.arange(L)
    qb = q.reshape(b, nb, BLOCK, h, dh).swapaxes(0, 1)
    Fq = Ft.reshape(b, h, nb, BLOCK).transpose(2, 0, 1, 3)
    qpos = pos.reshape(nb, BLOCK)
    kvalid = valid[None, :]

    def block(args):
        qi, fi, pi = args
        s = jnp.einsum('bqhd,bkhd->bhqk', qi, k).astype(F32) * (dh ** -0.5)
        s = s + fi[..., None] - Ft[:, :, None, :]
        mask = (pos[None, :] <= pi[:, None]) & kvalid
        p = jax.nn.softmax(jnp.where(mask, s, NEG), axis=-1)
        return jnp.einsum('bhqk,bkhd->bqhd', p.astype(v.dtype), v)

    o = lax.map(block, (qb, Fq, qpos))
    return o.swapaxes(0, 1).reshape(b, L, h, dh)


def _online(m, l, acc, s, v):
    m_new = jnp.maximum(m, s.max(-1))
    corr = jnp.exp(m - m_new)
    p = jnp.exp(s - m_new[..., None])
    l = l * corr + p.sum(-1)
    acc = acc * corr[..., None] + jnp.einsum('nhtp,nphd->nhtd', p, v.astype(F32))
    return m_new, l, acc


def _fox_sample(q, k, v, logf, cache_k, cache_v, cache_logf, page_table):
    n, T, h, dh = q.shape
    G = jnp.cumsum(logf, axis=1).transpose(0, 2, 1)

    def scores(kb, bias_k):
        s = jnp.einsum('nthd,nphd->nhtp', q, kb).astype(F32) * (dh ** -0.5)
        return s + G[..., None] + bias_k[:, :, None, :]

    def step(carry, phys):
        m, l, acc, suf = carry
        kp = cache_k[phys]
        vp = cache_v[phys]
        lf = cache_logf[phys].astype(F32).transpose(0, 2, 1)
        incl = jnp.cumsum(lf, axis=-1)
        tot = incl[..., -1]
        bias = suf[..., None] + tot[..., None] - incl
        m, l, acc = _online(m, l, acc, scores(kp, bias), vp)
        return (m, l, acc, suf + tot), None

    init = (jnp.full((n, h, T), NEG, F32), jnp.zeros((n, h, T), F32),
            jnp.zeros((n, h, T, dh), F32), jnp.zeros((n, h), F32))
    (m, l, acc, _), _ = lax.scan(step, init, jnp.flip(page_table, axis=1).T)
    causal = jnp.arange(T)[:, None] >= jnp.arange(T)[None, :]
    s_new = jnp.where(causal, scores(k, -G), NEG)
    m, l, acc = _online(m, l, acc, s_new, v)
    return (acc / l[..., None]).transpose(0, 2, 1, 3).astype(q.dtype)


def _hgrn_scan(q, k, lf, v, S0, chunk):
    n, L, h, _ = q.shape
    nc = L // chunk

    def blocks(a):
        return a.reshape(n, nc, chunk, *a.shape[2:]).swapaxes(0, 1)

    tri = jnp.arange(chunk)[:, None] >= jnp.arange(chunk)[None, :]
    mask = tri[None, :, :, None, None]

    def step(S, xs):
        qb, kb, lb, vb = xs
        b = jnp.cumsum(lb, axis=1)
        o_inter = jnp.einsum('nthk,nhkv->nthv', qb * jnp.exp(b), S)
        diff = b[:, :, None] - b[:, None, :]
        decay = jnp.where(mask, jnp.exp(jnp.where(mask, diff, 0.0)), 0.0)
        a = jnp.einsum('nthk,ntshk,nshk->nhts', qb, decay, kb)
        o_intra = jnp.einsum('nhts,nshv->nthv', a, vb)
        b_last = b[:, -1]
        S = jnp.exp(b_last)[..., None] * S + jnp.einsum('nshk,nshv->nhkv', kb * jnp.exp(b_last[:, None] - b), vb)
        return S, o_inter + o_intra

    S, o = lax.scan(step, S0, (blocks(q), blocks(k), blocks(lf), blocks(v)))
    return o.swapaxes(0, 1).reshape(n, L, h, -1), S


def _sink_attend(s, sink, v, spec):
    m = jnp.maximum(s.max(-1, keepdims=True), sink)
    p = jnp.exp(s - m)
    den = p.sum(-1, keepdims=True) + jnp.exp(sink - m)
    return jnp.einsum(spec, (p / den).astype(v.dtype), v)


def _swa_prompt(q, k, v, sinks):
    n, L = q.shape[:2]
    nb = L // BLOCK
    qb = q.reshape(n, nb, BLOCK, KV_D, G_D, DH_D)

    def band(a):
        ab = a.reshape(n, nb, BLOCK, KV_D, DH_D)
        prev = jnp.pad(ab, ((0, 0), (1, 0), (0, 0), (0, 0), (0, 0)))[:, :-1]
        return jnp.concatenate([prev, ab], axis=2)

    kk, vv = band(k), band(v)
    kj = jnp.arange(2 * BLOCK)
    dist = jnp.arange(BLOCK)[:, None] + BLOCK - kj[None, :]
    kpos = jnp.arange(nb)[:, None] * BLOCK - BLOCK + kj[None, :]
    mask = ((dist >= 0) & (dist <= WINDOW))[None] & (kpos >= PAD)[:, None, :]
    s = jnp.einsum('nbqkgd,nbskd->nbkgqs', qb, kk).astype(F32) * (DH_D ** -0.5)
    s = s - _alibi_slopes().reshape(KV_D, G_D)[:, :, None, None] * dist.astype(F32)
    s = jnp.where(mask[None, :, None, None], s, NEG)
    o = _sink_attend(s, sinks.astype(F32).reshape(KV_D, G_D, 1, 1), vv, 'nbkgqs,nbskd->nbqkgd')
    return o.reshape(n, L, H_D, DH_D)


def _swa_sample(q, k, v, buf_k, buf_v, sinks):
    n, T = q.shape[:2]
    wb = buf_k.shape[1]
    kk = jnp.concatenate([buf_k.astype(k.dtype), k], axis=1)
    vv = jnp.concatenate([buf_v.astype(v.dtype), v], axis=1)
    dist = jnp.arange(T)[:, None] + wb - jnp.arange(wb + T)[None, :]
    mask = (dist >= 0) & (dist <= WINDOW)
    s = jnp.einsum('ntkgd,nskd->nkgts', q.reshape(n, T, KV_D, G_D, DH_D), kk).astype(F32) * (DH_D ** -0.5)
    s = s - _alibi_slopes().reshape(KV_D, G_D)[:, :, None, None] * dist.astype(F32)
    s = jnp.where(mask, s, NEG)
    o = _sink_attend(s, sinks.astype(F32).reshape(KV_D, G_D, 1, 1), vv, 'nkgts,nskd->ntkgd')
    return o.reshape(n, T, H_D, DH_D), kk[:, -wb:], vv[:, -wb:]


def _even_proj(h, w_in, b_f):
    n, L, _ = h.shape
    u_a, g_a, q, k, v, g_b, f_logit = _split(h @ w_in, E_SPLIT)
    hd = (n, L, H_B, DH_B)
    logf = jax.nn.log_sigmoid((f_logit + b_f).astype(F32))
    return u_a, g_a, q.reshape(hd), k.reshape(hd), v.reshape(hd), g_b, logf


def _even_merge(y_a, g_a, y_b, g_b, w_out):
    y_b = y_b.reshape(g_b.shape)
    return jnp.concatenate([y_a * jax.nn.silu(g_a), y_b * jax.nn.silu(g_b)], -1) @ w_out


def _odd_proj(h, w_in, lb):
    n, L, _ = h.shape
    q_c, f_c, i_c, g_c, q_d, k_d, v_d, g_d = _split(h @ w_in, O_SPLIT)
    f = lb + (1.0 - lb) * jax.nn.sigmoid(f_c.astype(F32))
    ck = (n, L, H_C, DK_C)
    hg = (q_c.astype(F32).reshape(ck), (1.0 - f).reshape(ck), jnp.log(f).reshape(ck),
          i_c.astype(F32).reshape(n, L, H_C, DV_C))
    sw = (q_d.reshape(n, L, H_D, DH_D), k_d.reshape(n, L, KV_D, DH_D), v_d.reshape(n, L, KV_D, DH_D))
    return hg, sw, g_c, g_d


def _odd_merge(o_c, norm_g, g_c, y_d, g_d, w_out):
    n, L = g_c.shape[:2]
    y_c = o_c * lax.rsqrt(jnp.mean(o_c * o_c, -1, keepdims=True) + LN_EPS) * norm_g.astype(F32)
    y_c = y_c.reshape(n, L, D_C).astype(g_c.dtype)
    return jnp.concatenate([y_c * jax.nn.silu(g_c), y_d.reshape(n, L, D_D) * jax.nn.silu(g_d)], -1) @ w_out


def setup_inputs(seed: int = 0) -> dict:
    key = jax.random.key(seed)
    ks = iter(jax.random.split(key, 32))

    def nrm(shape, scale=1.0):
        return scale * jax.random.normal(next(ks), shape, F32)

    n_pages = PAST_LEN // PAGE_SIZE
    n_pool = (DEC_BATCH * n_pages * 5) // 4
    wb = min(WINDOW, PAST_LEN)
    perm = jax.random.permutation(next(ks), n_pool)
    page_table = perm[:DEC_BATCH * n_pages].reshape(DEC_BATCH, n_pages).astype(jnp.int32)
    d_mix = D_A + D_B
    d_mix_o = D_C + D_D
    fb = jnp.linspace(FORGET_BIAS_LO, FORGET_BIAS_HI, H_B, dtype=F32)
    b_f = fb[None, :] + nrm((N_EVEN, H_B), 0.1)
    cache_logf = jax.nn.log_sigmoid(b_f[:, None, None, :] + nrm((N_EVEN, n_pool, PAGE_SIZE, H_B)))
    return {
        'x_prompt': nrm((BATCH, SEQ, D_MODEL)),
        'x_sample': nrm((DEC_BATCH, DEC_SEQ, D_MODEL)),
        'cache_pool': nrm((N_EVEN, DEC_BATCH, POOL_BUF, D_A)),
        'cache_k': nrm((N_EVEN, n_pool, PAGE_SIZE, H_B, DH_B)),
        'cache_v': nrm((N_EVEN, n_pool, PAGE_SIZE, H_B, DH_B)),
        'cache_logf': cache_logf,
        'state_hgrn': nrm((N_ODD, DEC_BATCH, H_C, DK_C, DV_C), 0.5),
        'cache_win_k': nrm((N_ODD, DEC_BATCH, wb, KV_D, DH_D)),
        'cache_win_v': nrm((N_ODD, DEC_BATCH, wb, KV_D, DH_D)),
        'page_table': page_table,
        'meta_tokens': nrm((N_META, D_MODEL)),
        'w_in_e': nrm((N_EVEN, D_MODEL, IN_E), D_MODEL ** -0.5),
        'b_f': b_f,
        'w_pool': nrm((N_EVEN, N_POOL_GROUPS, GC, GC), GC ** -0.5),
        'pool_scale': 1.0 + nrm((N_EVEN, D_A), 0.02),
        'w_out_e': nrm((N_EVEN, d_mix, D_MODEL), BETA * d_mix ** -0.5),
        'ln_g_e': 1.0 + nrm((N_EVEN, D_MODEL), 0.02),
        'ln_b_e': nrm((N_EVEN, D_MODEL), 0.02),
        'w_in_o': nrm((N_ODD, D_MODEL, IN_O), D_MODEL ** -0.5),
        'hgrn_gamma': nrm((DEPTH, HK_C), 0.5),
        'hgrn_norm_g': 1.0 + nrm((N_ODD, DV_C), 0.02),
        'sinks': nrm((N_ODD, H_D), 0.5),
        'w_out_o': nrm((N_ODD, d_mix_o, D_MODEL), BETA * d_mix_o ** -0.5),
        'ln_g_o': 1.0 + nrm((N_ODD, D_MODEL), 0.02),
        'ln_b_o': nrm((N_ODD, D_MODEL), 0.02),
    }


def reference(x_prompt, x_sample, cache_pool, cache_k, cache_v, cache_logf, state_hgrn, cache_win_k,
              cache_win_v, page_table, meta_tokens, w_in_e, b_f, w_pool, pool_scale, w_out_e, ln_g_e,
              ln_b_e, w_in_o, hgrn_gamma, hgrn_norm_g, sinks, w_out_o, ln_g_o, ln_b_o):
    gam = jax.nn.softmax(hgrn_gamma.astype(F32), axis=0)
    lower_bounds = jnp.cumsum(gam, axis=0) - gam[0]
    wb = cache_win_k.shape[2]
    n_p = x_prompt.shape[0]
    LP = x_prompt.shape[1] + BLOCK
    valid = jnp.arange(LP) >= PAD
    dt = x_prompt.dtype
    hp = jnp.concatenate([jnp.zeros((n_p, PAD, D_MODEL), dt),
                          jnp.broadcast_to(meta_tokens.astype(dt), (n_p, N_META, D_MODEL)), x_prompt], axis=1)
    hs = x_sample
    pool_p, pool_s, k_p, v_p, lf_p, k_s, v_s, lf_s = [], [], [], [], [], [], [], []
    hg_p, hg_s, wk_p, wv_p, wk_s, wv_s = [], [], [], [], [], []
    for layer in range(DEPTH):
        li = layer // 2
        if layer % 2 == 0:
            u_a, g_a, q, k, v, g_b, logf = _even_proj(hp, w_in_e[li], b_f[li])
            logf = jnp.where(valid[None, :, None], logf, 0.0)
            y_a = _pool_mix(u_a, jnp.broadcast_to(valid, u_a.shape[:2]), w_pool[li], pool_scale[li])
            y_b = _fox_prompt(q, k, v, logf, valid)
            out_p = _even_merge(y_a, g_a, y_b, g_b, w_out_e[li])
            pool_p.append(u_a[:, -POOL_BUF:])
            k_p.append(k[:, PAD:])
            v_p.append(v[:, PAD:])
            lf_p.append(logf[:, PAD:].astype(dt))
            u_a, g_a, q, k, v, g_b, logf = _even_proj(hs, w_in_e[li], b_f[li])
            u_ext = jnp.concatenate([cache_pool[li].astype(u_a.dtype), u_a], axis=1)
            y_a = _pool_mix(u_ext, jnp.ones(u_ext.shape[:2], bool), w_pool[li], pool_scale[li])[:, POOL_BUF:]
            y_b = _fox_sample(q, k, v, logf, cache_k[li], cache_v[li], cache_logf[li], page_table)
            out_s = _even_merge(y_a, g_a, y_b, g_b, w_out_e[li])
            pool_s.append(u_ext[:, -POOL_BUF:])
            k_s.append(k)
            v_s.append(v)
            lf_s.append(logf.astype(hs.dtype))
            g_ln, b_ln = ln_g_e[li], ln_b_e[li]
        else:
            (qc, kc, lfc, vc), (qd, kd, vd), g_c, g_d = _odd_proj(hp, w_in_o[li], lower_bounds[layer])
            vm = valid[None, :, None, None]
            kc = jnp.where(vm, kc, 0.0)
            lfc = jnp.where(vm, lfc, 0.0)
            o_c, S = _hgrn_scan(qc, kc, lfc, vc, jnp.zeros((n_p, H_C, DK_C, DV_C), F32), HGRN_CHUNK)
            y_d = _swa_prompt(qd, kd, vd, sinks[li])
            out_p = _odd_merge(o_c, hgrn_norm_g[li], g_c, y_d, g_d, w_out_o[li])
            hg_p.append(S.astype(dt))
            wk_p.append(kd[:, -wb:])
            wv_p.append(vd[:, -wb:])
            (qc, kc, lfc, vc), (qd, kd, vd), g_c, g_d = _odd_proj(hs, w_in_o[li], lower_bounds[layer])
            o_c, S = _hgrn_scan(qc, kc, lfc, vc, state_hgrn[li].astype(F32), hs.shape[1])
            y_d, nk, nv = _swa_sample(qd, kd, vd, cache_win_k[li], cache_win_v[li], sinks[li])
            out_s = _odd_merge(o_c, hgrn_norm_g[li], g_c, y_d, g_d, w_out_o[li])
            hg_s.append(S.astype(hs.dtype))
            wk_s.append(nk)
            wv_s.append(nv)
            g_ln, b_ln = ln_g_o[li], ln_b_o[li]
        hp = jnp.where(valid[None, :, None], _layer_norm(ALPHA * hp + out_p, g_ln, b_ln), 0.0).astype(dt)
        hs = _layer_norm(ALPHA * hs + out_s, g_ln, b_ln)
    y_prompt = hp[:, BLOCK:]
    return (y_prompt, hs, jnp.stack(pool_p), jnp.stack(pool_s), jnp.stack(k_p), jnp.stack(v_p),
            jnp.stack(lf_p), jnp.stack(k_s), jnp.stack(v_s), jnp.stack(lf_s), jnp.stack(hg_p),
            jnp.stack(hg_s), jnp.stack(wk_p), jnp.stack(wv_p), jnp.stack(wk_s), jnp.stack(wv_s))
```

```python
import functools

import numpy as np
import jax
import jax.numpy as jnp
from jax import lax
from jax.experimental import pallas as pl
from jax.experimental.pallas import tpu as pltpu

F32 = jnp.float32
BF16 = jnp.bfloat16

D_MODEL = 1024
DEPTH = 2
N_META = 16
BLOCK = 128
PAD = BLOCK - N_META
D_A = D_MODEL // 2
POOL_WINDOWS = (2, 4, 8, 16)
GC = D_A // len(POOL_WINDOWS)
W_MAX = 16
POOL_BUF = W_MAX - 1
H_B = D_MODEL // 128
DH_B = 64
D_B = H_B * DH_B
H_C = D_MODEL // 256
DK_C = 128
DV_C = 128
HK_C = H_C * DK_C
D_C = H_C * DV_C
HGRN_CHUNK = 64
H_D = D_MODEL // 128
KV_D = H_D // 4
G_D = H_D // KV_D
DH_D = 64
D_D = H_D * DH_D
WINDOW = 128
ALPHA = (2 * DEPTH) ** 0.25
LN_EPS = 1e-5
NEG = -1e30

LANES = 128
SUBLANES = 8
VMEM_LIMIT = 48 * 1024 * 1024

_NT = (((1,), (1,)), ((), ()))


def _silu(x):
    return x * (1.0 / (1.0 + jnp.exp(-x)))


def _split3(x):
    hi = x.astype(BF16).astype(F32)
    r = x - hi
    mid = r.astype(BF16).astype(F32)
    return hi, mid, r - mid


def _dot(a, b):
    return jnp.dot(a, b, preferred_element_type=F32)


def _dot_nt(a, b):
    return lax.dot_general(a, b, _NT, preferred_element_type=F32)


def _dot3(m, x):
    hi, mid, lo = _split3(x)
    return _dot(m, hi) + _dot(m, mid) + _dot(m, lo)


def _params(sem):
    return pltpu.CompilerParams(dimension_semantics=sem, vmem_limit_bytes=VMEM_LIMIT)


def _proj_even_kernel(x_ref, w_ref, wf_ref, bf_ref, u_ref, ga_ref, q_ref, k_ref, v_ref, gb_ref,
                      lf_ref, *, tm, first_valid):
    xb = x_ref[...].astype(BF16)
    for i, o_ref in enumerate((u_ref, ga_ref, q_ref, k_ref, v_ref, gb_ref)):
        o_ref[...] = _dot(xb, w_ref[:, i * 512:(i + 1) * 512])
    z = _dot(xb, wf_ref[...]) + bf_ref[...]
    lf = jnp.minimum(z, 0.0) - jnp.log1p(jnp.exp(-jnp.abs(z)))
    pos = pl.program_id(1) * tm + lax.broadcasted_iota(jnp.int32, lf.shape, 0)
    lane = lax.broadcasted_iota(jnp.int32, lf.shape, 1)
    lf_ref[...] = jnp.where((pos >= first_valid) & (lane < H_B), lf, 0.0)


def _proj_even(x, w6, wf, bf, *, nb, seq, tm, first_valid):
    nt = seq // tm
    rows = nb * seq
    row_spec = lambda c: pl.BlockSpec((tm, c), lambda b, j: (b * nt + j, 0))
    full = lambda a: pl.BlockSpec(a.shape, lambda b, j: (0,) * a.ndim)
    outs = [jax.ShapeDtypeStruct((rows, 512), F32)] * 6 + [jax.ShapeDtypeStruct((rows, LANES), F32)]
    return pl.pallas_call(
        functools.partial(_proj_even_kernel, tm=tm, first_valid=first_valid),
        grid=(nb, nt),
        in_specs=[row_spec(D_MODEL), full(w6), full(wf), full(bf)],
        out_specs=[row_spec(512)] * 6 + [row_spec(LANES)],
        out_shape=outs,
        compiler_params=_params(("parallel", "parallel")),
        name="proj_even",
    )(x, w6, wf, bf)


def _proj_odd_kernel(x_ref, w_ref, gam_ref, qc_ref, kc_ref, lf_ref, vc_ref, gc_ref, qd_ref, kd_ref,
                     vd_ref, gd_ref, *, tm, first_valid):
    xb = x_ref[...].astype(BF16)
    g0 = gam_ref[0:1, :]
    g1 = gam_ref[1:2, :]
    mx = jnp.maximum(g0, g1)
    e0 = jnp.exp(g0 - mx)
    e1 = jnp.exp(g1 - mx)
    p0 = e0 / (e0 + e1)
    p1 = e1 / (e0 + e1)
    lb = (p0 + p1) - p0
    pos = pl.program_id(1) * tm + lax.broadcasted_iota(jnp.int32, (tm, 1), 0)
    valid = pos >= first_valid
    qc_ref[...] = _dot(xb, w_ref[:, 0:512])
    fc = _dot(xb, w_ref[:, 512:1024])
    f = lb + (1.0 - lb) * (1.0 / (1.0 + jnp.exp(-fc)))
    kc_ref[...] = jnp.where(valid, 1.0 - f, 0.0)
    lf_ref[...] = jnp.where(valid, jnp.log(f), 0.0)
    vc_ref[...] = _dot(xb, w_ref[:, 1024:1536])
    gc_ref[...] = _dot(xb, w_ref[:, 1536:2048])
    qd_ref[...] = _dot(xb, w_ref[:, 2048:2560])
    kd_ref[...] = _dot(xb, w_ref[:, 2560:2688])
    vd_ref[...] = _dot(xb, w_ref[:, 2688:2816])
    gd_ref[...] = _dot(xb, w_ref[:, 2816:3328])


def _proj_odd(x, w, gam, *, nb, seq, tm, first_valid):
    nt = seq // tm
    rows = nb * seq
    row_spec = lambda c: pl.BlockSpec((tm, c), lambda b, j: (b * nt + j, 0))
    full = lambda a: pl.BlockSpec(a.shape, lambda b, j: (0,) * a.ndim)
    widths = (512, 512, 512, 512, 512, 512, 128, 128, 512)
    return pl.pallas_call(
        functools.partial(_proj_odd_kernel, tm=tm, first_valid=first_valid),
        grid=(nb, nt),
        in_specs=[row_spec(D_MODEL), full(w), full(gam)],
        out_specs=[row_spec(c) for c in widths],
        out_shape=[jax.ShapeDtypeStruct((rows, c), F32) for c in widths],
        compiler_params=_params(("parallel", "parallel")),
        name="proj_odd",
    )(x, w, gam)


def _out_ln_kernel(a_ref, b_ref, x_ref, wa_ref, wb_ref, g_ref, bb_ref, o_ref, *, tm, first_valid):
    acc = _dot(a_ref[...].astype(BF16), wa_ref[...]) + _dot(b_ref[...].astype(BF16), wb_ref[...])
    y = ALPHA * x_ref[...] + acc
    mu = jnp.mean(y, axis=-1, keepdims=True)
    yc = y - mu
    var = jnp.mean(yc * yc, axis=-1, keepdims=True)
    o = yc * lax.rsqrt(var + LN_EPS) * g_ref[...] + bb_ref[...]
    pos = pl.program_id(1) * tm + lax.broadcasted_iota(jnp.int32, (tm, 1), 0)
    o_ref[...] = jnp.where(pos >= first_valid, o, 0.0)


def _out_ln(a, b, x, wa, wb, g, bb, *, nb, seq, tm, first_valid):
    nt = seq // tm
    row_spec = lambda c: pl.BlockSpec((tm, c), lambda i, j: (i * nt + j, 0))
    full = lambda t: pl.BlockSpec(t.shape, lambda i, j: (0,) * t.ndim)
    return pl.pallas_call(
        functools.partial(_out_ln_kernel, tm=tm, first_valid=first_valid),
        grid=(nb, nt),
        in_specs=[row_spec(512), row_spec(512), row_spec(D_MODEL), full(wa), full(wb), full(g), full(bb)],
        out_specs=row_spec(D_MODEL),
        out_shape=jax.ShapeDtypeStruct(x.shape, F32),
        compiler_params=_params(("parallel", "parallel")),
        name="out_ln",
    )(a, b, x, wa, wb, g, bb)


def _pool_kernel(u_ref, g_ref, wp_ref, sc_ref, y_ref, buf_ref, *, tr, first_valid):
    j = pl.program_id(1)
    pos = j * tr + lax.broadcasted_iota(jnp.int32, (tr, 1), 0)

    @pl.when(j == 0)
    def _():
        buf_ref[0:W_MAX, :] = jnp.zeros((W_MAX, D_A), F32)

    @pl.when(j > 0)
    def _():
        buf_ref[0:W_MAX, :] = buf_ref[tr:tr + W_MAX, :]

    uf = jnp.where(pos >= first_valid, u_ref[...], 0.0)
    buf_ref[W_MAX:W_MAX + tr, :] = uf
    for g, w in enumerate(POOL_WINDOWS):
        cs = slice(g * GC, (g + 1) * GC)
        s = buf_ref[W_MAX:W_MAX + tr, cs]
        for i in range(1, w):
            s = s + buf_ref[W_MAX - i:W_MAX - i + tr, cs]
        cnt = jnp.clip(pos - first_valid + 1, 1, w).astype(F32)
        d = s / cnt - uf[:, cs]
        y = _dot(d, wp_ref[g]) * sc_ref[:, cs]
        y_ref[:, cs] = y * _silu(g_ref[:, cs])


def _pool(u, g, wp, sc, *, nb, seq, tr, first_valid):
    nt = seq // tr
    row_spec = pl.BlockSpec((tr, D_A), lambda b, j: (b * nt + j, 0))
    return pl.pallas_call(
        functools.partial(_pool_kernel, tr=tr, first_valid=first_valid),
        grid=(nb, nt),
        in_specs=[row_spec, row_spec,
                  pl.BlockSpec(wp.shape, lambda b, j: (0, 0, 0)),
                  pl.BlockSpec(sc.shape, lambda b, j: (0, 0))],
        out_specs=row_spec,
        out_shape=jax.ShapeDtypeStruct(u.shape, F32),
        scratch_shapes=[pltpu.VMEM((tr + W_MAX, D_A), F32)],
        compiler_params=_params(("parallel", "arbitrary")),
        name="pool",
    )(u, g, wp, sc)


def _fcum_kernel(lf_ref, aq_ref, ak_ref, carry_ref):
    @pl.when(pl.program_id(1) == 0)
    def _():
        carry_ref[...] = jnp.zeros(carry_ref.shape, F32)

    t = lax.broadcasted_iota(jnp.int32, (BLOCK, BLOCK), 0)
    u = lax.broadcasted_iota(jnp.int32, (BLOCK, BLOCK), 1)
    tri = jnp.where(t >= u, 1.0, 0.0).astype(F32)
    f = _dot3(tri, lf_ref[...]) + carry_ref[...]
    carry_ref[...] = f[BLOCK - 1:BLOCK, :]
    fh, fm, fl = _split3(f)
    f3 = fh + pltpu.roll(fm, H_B, 1) + pltpu.roll(fl, 2 * H_B, 1)
    lane = u
    aq_ref[...] = jnp.where((lane >= 3 * H_B) & (lane < 6 * H_B), 1.0, f3)
    ak_ref[...] = jnp.where(lane < 3 * H_B, 1.0, -pltpu.roll(f3, 3 * H_B, 1))


def _fcum(lf, *, nb, seq):
    nt = seq // BLOCK
    spec = pl.BlockSpec((BLOCK, LANES), lambda b, j: (b * nt + j, 0))
    sds = jax.ShapeDtypeStruct(lf.shape, F32)
    return pl.pallas_call(
        _fcum_kernel,
        grid=(nb, nt),
        in_specs=[spec],
        out_specs=[spec, spec],
        out_shape=[sds, sds],
        scratch_shapes=[pltpu.VMEM((1, LANES), F32)],
        compiler_params=_params(("parallel", "arbitrary")),
        name="fcum",
    )(lf)


def _fox_prompt_kernel(q_ref, k_ref, v_ref, aq_ref, ak_ref, gb_ref, y_ref, *, seq, first_valid):
    i = pl.program_id(1)
    lane = lax.broadcasted_iota(jnp.int32, (BLOCK, LANES), 1)
    low = lane < DH_B
    row = lax.broadcasted_iota(jnp.int32, (2 * BLOCK, 1), 0)
    qpos = i * BLOCK + (row & (BLOCK - 1))
    kpos = lax.broadcasted_iota(jnp.int32, (1, seq), 1)
    mask = (kpos <= qpos) & (kpos >= first_valid)
    aq = aq_ref[...]
    ak = ak_ref[...]
    for j in range(H_B // 2):
        cs = slice(j * LANES, (j + 1) * LANES)
        qp = q_ref[:, cs] * (DH_B ** -0.5)
        a0 = jnp.where(((lane & (H_B - 1)) == 2 * j) & (lane < 6 * H_B), aq, 0.0)
        a1 = jnp.where(((lane & (H_B - 1)) == 2 * j + 1) & (lane < 6 * H_B), aq, 0.0)
        qs = jnp.concatenate([
            jnp.concatenate([jnp.where(low, qp, 0.0), a0], axis=1),
            jnp.concatenate([jnp.where(low, 0.0, qp), a1], axis=1)], axis=0)
        ks = jnp.concatenate([k_ref[:, cs], ak], axis=1)
        s = jnp.where(mask, _dot_nt(qs, ks), NEG)
        m = jnp.max(s, axis=-1, keepdims=True)
        p = jnp.exp(s - m)
        l = jnp.sum(p, axis=-1, keepdims=True)
        r = _dot(p, v_ref[:, cs]) / l
        o = jnp.where(low, r[:BLOCK], r[BLOCK:])
        y_ref[:, cs] = o * _silu(gb_ref[:, cs])


def _fox_prompt(q, k, v, aq, ak, gb, *, nb, seq, first_valid):
    nt = seq // BLOCK
    blk = lambda c: pl.BlockSpec((BLOCK, c), lambda b, i: (b * nt + i, 0))
    whole = lambda c: pl.BlockSpec((seq, c), lambda b, i: (b, 0))
    return pl.pallas_call(
        functools.partial(_fox_prompt_kernel, seq=seq, first_valid=first_valid),
        grid=(nb, nt),
        in_specs=[blk(D_B), whole(D_B), whole(D_B), blk(LANES), whole(LANES), blk(D_B)],
        out_specs=blk(D_B),
        out_shape=jax.ShapeDtypeStruct(q.shape, F32),
        compiler_params=_params(("parallel", "parallel")),
        name="fox_prompt",
    )(q, k, v, aq, ak, gb)


def _fox_sample_kernel(pt_ref, q_ref, kn_ref, vn_ref, lft_ref, gb_ref, ut_ref, *rest, gp):
    kt_refs = rest[0:gp]
    vt_refs = rest[gp:2 * gp]
    lf_refs = rest[2 * gp:3 * gp]
    y_ref = rest[3 * gp]
    qbd_ref, m_ref, l_ref, acc_ref, suf_ref = rest[3 * gp + 1:]
    step = pl.program_id(1)
    lane512 = lax.broadcasted_iota(jnp.int32, (SUBLANES, D_B), 1)
    nrow = H_B * SUBLANES

    @pl.when(step == 0)
    def _():
        qs = q_ref[...] * (DH_B ** -0.5)
        for h in range(H_B):
            qbd_ref[h * 8:(h + 1) * 8, :] = jnp.where((lane512 >> 6) == h, qs, 0.0)
        lft = lft_ref[0]
        lane = lax.broadcasted_iota(jnp.int32, (H_B, LANES), 1)
        gcum = jnp.zeros((H_B, LANES), F32)
        for u in range(SUBLANES):
            gcum = gcum + jnp.where(lane >= u, jnp.broadcast_to(lft[:, u:u + 1], (H_B, LANES)), 0.0)
        kpad = jnp.concatenate([kn_ref[...], jnp.zeros((LANES - 8, D_B), F32)], axis=0)
        vpad = jnp.concatenate([vn_ref[...], jnp.zeros((LANES - 8, D_B), F32)], axis=0)
        s = _dot_nt(qbd_ref[...], kpad)
        bias = jnp.concatenate(
            [jnp.broadcast_to(gcum[h:h + 1, :], (8, LANES)) for h in range(H_B)], axis=0)
        t = lax.broadcasted_iota(jnp.int32, (nrow, LANES), 0) & 7
        sl = lax.broadcasted_iota(jnp.int32, (nrow, LANES), 1)
        s = jnp.where(sl <= t, s - bias, NEG)
        m = jnp.max(s, axis=-1, keepdims=True)
        p = jnp.exp(s - m)
        m_ref[...] = m
        l_ref[...] = jnp.sum(p, axis=-1, keepdims=True)
        acc_ref[...] = _dot(p, vpad)
        suf_ref[...] = jnp.zeros(suf_ref.shape, F32)

    lfs = jnp.concatenate([r[0, 0] for r in lf_refs], axis=0)
    wt = _dot3_rhs(lfs, ut_ref[...])
    suf = suf_ref[...]
    biases = [None] * gp
    for g in range(gp - 1, -1, -1):
        biases[g] = wt[g * 8:(g + 1) * 8, :LANES] + suf
        suf = suf + wt[g * 8:(g + 1) * 8, LANES:]
    suf_ref[...] = suf

    qbd = qbd_ref[...]
    ss = []
    for g in range(gp):
        sg = _dot(qbd, kt_refs[g][0, 0])
        bg = jnp.concatenate(
            [jnp.broadcast_to(biases[g][h:h + 1, :], (8, LANES)) for h in range(H_B)], axis=0)
        ss.append(sg + bg)
    m_prev = m_ref[...]
    m_cur = ss[0]
    for g in range(1, gp):
        m_cur = jnp.maximum(m_cur, ss[g])
    m_new = jnp.maximum(m_prev, jnp.max(m_cur, axis=-1, keepdims=True))
    alpha = jnp.exp(m_prev - m_new)
    psum = None
    pv = None
    for g in range(gp):
        p = jnp.exp(ss[g] - m_new)
        psum = p if psum is None else psum + p
        d = _dot_nt(p, vt_refs[g][0, 0])
        pv = d if pv is None else pv + d
    m_ref[...] = m_new
    l_ref[...] = alpha * l_ref[...] + jnp.sum(psum, axis=-1, keepdims=True)
    acc_ref[...] = alpha * acc_ref[...] + pv

    @pl.when(step == pl.num_programs(1) - 1)
    def _():
        o = acc_ref[...] / l_ref[...]
        out = jnp.zeros((SUBLANES, D_B), F32)
        for h in range(H_B):
            out = out + jnp.where((lane512 >> 6) == h, o[h * 8:(h + 1) * 8, :], 0.0)
        y_ref[...] = out * _silu(gb_ref[...])


def _dot3_rhs(x, m):
    hi, mid, lo = _split3(x)
    return _dot(hi, m) + _dot(mid, m) + _dot(lo, m)


def _fox_sample(page_table, q, kn, vn, lft, gb, kt, vt, lfc, *, gp):
    n, npages = page_table.shape
    steps = npages // gp
    s_i = np.arange(LANES)
    ut = np.concatenate([(s_i[:, None] > s_i[None, :]).astype(np.float32),
                         np.ones((LANES, LANES), np.float32)], axis=1)

    def page_map(g):
        return lambda i, s, pt: (0, pt[i, npages - (s + 1) * gp + g], 0, 0)

    row = lambda c: pl.BlockSpec((8, c), lambda i, s, pt: (i, 0))
    in_specs = [row(D_B), row(D_B), row(D_B),
                pl.BlockSpec((1, H_B, LANES), lambda i, s, pt: (i, 0, 0)),
                row(D_B),
                pl.BlockSpec(ut.shape, lambda i, s, pt: (0, 0))]
    in_specs += [pl.BlockSpec((1, 1, D_B, LANES), page_map(g)) for g in range(gp)]
    in_specs += [pl.BlockSpec((1, 1, D_B, LANES), page_map(g)) for g in range(gp)]
    in_specs += [pl.BlockSpec((1, 1, H_B, LANES), page_map(g)) for g in range(gp)]
    nrow = H_B * SUBLANES
    grid_spec = pltpu.PrefetchScalarGridSpec(
        num_scalar_prefetch=1,
        grid=(n, steps),
        in_specs=in_specs,
        out_specs=pl.BlockSpec((8, D_B), lambda i, s, pt: (i, 0)),
        scratch_shapes=[pltpu.VMEM((nrow, D_B), F32), pltpu.VMEM((nrow, 1), F32),
                        pltpu.VMEM((nrow, 1), F32), pltpu.VMEM((nrow, D_B), F32),
                        pltpu.VMEM((H_B, LANES), F32)])
    return pl.pallas_call(
        functools.partial(_fox_sample_kernel, gp=gp),
        grid_spec=grid_spec,
        out_shape=jax.ShapeDtypeStruct(q.shape, F32),
        compiler_params=_params(("arbitrary", "arbitrary")),
        name="fox_sample",
    )(page_table, q, kn, vn, lft, gb, jnp.asarray(ut), *([kt] * gp), *([vt] * gp), *([lfc] * gp))


def _hgrn_levels(c):
    out = []
    m = c // 2
    while m >= 1:
        out.append(m)
        m //= 2
    return tuple(out)


def _hgrn_mstack(c):
    t = np.arange(c)[:, None]
    u = np.arange(c)[None, :]
    mats = [(u <= t)]
    for m in _hgrn_levels(c):
        r = (t // (2 * m)) * (2 * m) + m - 1
        second = (t % (2 * m)) >= m
        mats.append(np.where(second, (u > r) & (u <= t), (u > t) & (u <= r)))
    return np.concatenate(mats, axis=0).astype(np.float32)


def _hgrn_kernel(*refs, c, has_s0):
    if has_s0:
        q_ref, k_ref, lf_ref, v_ref, g_ref, m_ref, ng_ref, s0_ref, y_ref, so_ref, st_ref = refs
    else:
        q_ref, k_ref, lf_ref, v_ref, g_ref, m_ref, ng_ref, y_ref, so_ref, st_ref = refs
    ci = pl.program_id(1)
    levels = _hgrn_levels(c)

    @pl.when(ci == 0)
    def _():
        for h in range(H_C):
            st_ref[h] = s0_ref[0, h].T if has_s0 else jnp.zeros((DV_C, DK_C), F32)

    t_col = lax.broadcasted_iota(jnp.int32, (c, 1), 0)
    t_row = lax.broadcasted_iota(jnp.int32, (c, c), 0)
    s_col = lax.broadcasted_iota(jnp.int32, (c, c), 1)
    rpad = max(LANES - c, 0)
    for h in range(H_C):
        cs = slice(h * DK_C, (h + 1) * DK_C)
        e = _dot3(m_ref[...], lf_ref[:, cs])
        b = e[0:c]
        q = q_ref[:, cs]
        k = k_ref[:, cs]
        v = v_ref[:, cs]
        st = st_ref[h]
        o = _dot_nt(q * jnp.exp(b), st)
        a = jnp.where(t_row == s_col, jnp.sum(q * k, axis=-1, keepdims=True), 0.0)
        for li, m in enumerate(levels):
            x = jnp.exp(e[(li + 1) * c:(li + 2) * c])
            second = (t_col & (2 * m - 1)) >= m
            am = _dot_nt(jnp.where(second, q * x, 0.0), jnp.where(second, 0.0, k * x))
            sh = (2 * m).bit_length() - 1
            a = a + jnp.where((t_row >> sh) == (s_col >> sh), am, 0.0)
        o = o + _dot(a, v)
        bl = b[c - 1:c, :]
        kp = k * jnp.exp(bl - b)
        if rpad:
            vp = jnp.concatenate([v, jnp.zeros((rpad, DV_C), F32)], axis=0)
            kp = jnp.concatenate([kp, jnp.zeros((rpad, DK_C), F32)], axis=0)
        else:
            vp = v
        st_ref[h] = st * jnp.exp(bl) + _dot(vp.T, kp)
        y = o * lax.rsqrt(jnp.mean(o * o, axis=-1, keepdims=True) + LN_EPS) * ng_ref[...]
        y_ref[:, cs] = y * _silu(g_ref[:, cs])

    @pl.when(ci == pl.num_programs(1) - 1)
    def _():
        for h in range(H_C):
            so_ref[0, h] = st_ref[h].T


def _hgrn(q, k, lf, v, g, ng, s0, *, nb, seq, c):
    nc = seq // c
    ms = jnp.asarray(_hgrn_mstack(c))
    blk = pl.BlockSpec((c, HK_C), lambda b, j: (b * nc + j, 0))
    st_spec = pl.BlockSpec((1, H_C, DK_C, DV_C), lambda b, j: (b, 0, 0, 0))
    in_specs = [blk, blk, blk, blk, blk,
                pl.BlockSpec(ms.shape, lambda b, j: (0, 0)),
                pl.BlockSpec(ng.shape, lambda b, j: (0, 0))]
    args = [q, k, lf, v, g, ms, ng]
    if s0 is not None:
        in_specs.append(st_spec)
        args.append(s0)
    return pl.pallas_call(
        functools.partial(_hgrn_kernel, c=c, has_s0=s0 is not None),
        grid=(nb, nc),
        in_specs=in_specs,
        out_specs=[blk, st_spec],
        out_shape=[jax.ShapeDtypeStruct(q.shape, F32),
                   jax.ShapeDtypeStruct((nb, H_C, DK_C, DV_C), F32)],
        scratch_shapes=[pltpu.VMEM((H_C, DV_C, DK_C), F32)],
        compiler_params=_params(("parallel", "arbitrary")),
        name="hgrn",
    )(*args)


def _swa_prompt_kernel(sk_ref, q_ref, kp_ref, kc_ref, vp_ref, vc_ref, g_ref, y_ref, *, first_valid):
    i = pl.program_id(1)
    lane = lax.broadcasted_iota(jnp.int32, (BLOCK, LANES), 1)
    low = lane < DH_D
    kk = jnp.concatenate([kp_ref[...], kc_ref[...]], axis=0)
    vv = jnp.concatenate([vp_ref[...], vc_ref[...]], axis=0)
    kk_sw = pltpu.roll(kk, DH_D, 1)
    vv_sw = pltpu.roll(vv, DH_D, 1)
    kj = lax.broadcasted_iota(jnp.int32, (BLOCK, 2 * BLOCK), 1)
    tq = lax.broadcasted_iota(jnp.int32, (BLOCK, 2 * BLOCK), 0)
    dist = tq + BLOCK - kj
    kpos = i * BLOCK - BLOCK + kj
    mask = (dist >= 0) & (dist <= WINDOW) & (kpos >= first_valid)
    distf = dist.astype(F32)
    for j in range(H_D // 2):
        cs = slice(j * LANES, (j + 1) * LANES)
        qp = q_ref[:, cs] * (DH_D ** -0.5)
        outs = []
        for e in range(2):
            h = 2 * j + e
            kvh = h // G_D
            qm = jnp.where(low, qp, 0.0) if e == 0 else jnp.where(low, 0.0, qp)
            s = _dot_nt(qm, kk if e == kvh else kk_sw) - (2.0 ** -(h + 1)) * distf
            s = jnp.where(mask, s, NEG)
            sink = sk_ref[h]
            m = jnp.maximum(jnp.max(s, axis=-1, keepdims=True), sink)
            p = jnp.exp(s - m)
            den = jnp.sum(p, axis=-1, keepdims=True) + jnp.exp(sink - m)
            outs.append(_dot(p, vv if e == kvh else vv_sw) / den)
        y_ref[:, cs] = jnp.where(low, outs[0], outs[1]) * _silu(g_ref[:, cs])


def _swa_prompt(sinks, q, k, v, g, *, nb, seq, first_valid):
    nt = seq // BLOCK
    cur = lambda c: pl.BlockSpec((BLOCK, c), lambda b, i: (b * nt + i, 0))
    prev = lambda c: pl.BlockSpec((BLOCK, c), lambda b, i: (b * nt + jnp.maximum(i - 1, 0), 0))
    kvw = KV_D * DH_D
    return pl.pallas_call(
        functools.partial(_swa_prompt_kernel, first_valid=first_valid),
        grid=(nb, nt),
        in_specs=[pl.BlockSpec(memory_space=pltpu.SMEM),
                  cur(D_D), prev(kvw), cur(kvw), prev(kvw), cur(kvw), cur(D_D)],
        out_specs=cur(D_D),
        out_shape=jax.ShapeDtypeStruct(q.shape, F32),
        compiler_params=_params(("parallel", "parallel")),
        name="swa_prompt",
    )(sinks, q, k, k, v, v, g)


def _swa_sample_kernel(sk_ref, q_ref, kn_ref, vn_ref, kt_ref, vt_ref, g_ref, y_ref):
    nrow = H_D * SUBLANES
    lane8 = lax.broadcasted_iota(jnp.int32, (SUBLANES, LANES), 1)
    blocks = []
    for h in range(H_D):
        j, e, kvh = h // 2, h % 2, h // G_D
        qp = q_ref[:, j * LANES:(j + 1) * LANES] * (DH_D ** -0.5)
        src = qp if e == kvh else pltpu.roll(qp, DH_D, 1)
        blocks.append(jnp.where((lane8 >> 6) == kvh, src, 0.0))
    qbd = jnp.concatenate(blocks, axis=0)
    rows = lax.broadcasted_iota(jnp.int32, (nrow, 1), 0)
    hrow = rows >> 3
    t = rows & 7
    slope = jnp.zeros((nrow, 1), F32)
    sink = jnp.zeros((nrow, 1), F32)
    for h in range(H_D):
        slope = jnp.where(hrow == h, 2.0 ** -(h + 1), slope)
        sink = jnp.where(hrow == h, sk_ref[h], sink)
    kj = lax.broadcasted_iota(jnp.int32, (nrow, LANES), 1)
    kpad = jnp.concatenate([kn_ref[...], jnp.zeros((LANES - 8, LANES), F32)], axis=0)
    vpad = jnp.concatenate([vn_ref[...], jnp.zeros((LANES - 8, LANES), F32)], axis=0)
    d_old = t + WINDOW - kj
    s_old = _dot(qbd, kt_ref[0, 0]) - slope * d_old.astype(F32)
    s_old = jnp.where(d_old <= WINDOW, s_old, NEG)
    d_new = t - kj
    s_new = _dot_nt(qbd, kpad) - slope * d_new.astype(F32)
    s_new = jnp.where(d_new >= 0, s_new, NEG)
    m = jnp.maximum(jnp.maximum(jnp.max(s_old, axis=-1, keepdims=True),
                                jnp.max(s_new, axis=-1, keepdims=True)), sink)
    p_old = jnp.exp(s_old - m)
    p_new = jnp.exp(s_new - m)
    den = (jnp.sum(p_old, axis=-1, keepdims=True) + jnp.sum(p_new, axis=-1, keepdims=True)
           + jnp.exp(sink - m))
    o = (_dot_nt(p_old, vt_ref[0, 0]) + _dot(p_new, vpad)) / den
    o_sw = pltpu.roll(o, DH_D, 1)
    for j in range(H_D // 2):
        parts = []
        for e in range(2):
            h = 2 * j + e
            parts.append((o if e == h // G_D else o_sw)[h * 8:(h + 1) * 8, :])
        cs = slice(j * LANES, (j + 1) * LANES)
        y_ref[:, cs] = jnp.where(lane8 < DH_D, parts[0], parts[1]) * _silu(g_ref[:, cs])


def _swa_sample(sinks, q, kn, vn, kt, vt, g):
    n = q.shape[0] // 8
    row = lambda c: pl.BlockSpec((8, c), lambda i: (i, 0))
    buf = pl.BlockSpec((1, 1, LANES, LANES), lambda i: (0, i, 0, 0))
    kvw = KV_D * DH_D
    return pl.pallas_call(
        _swa_sample_kernel,
        grid=(n,),
        in_specs=[pl.BlockSpec(memory_space=pltpu.SMEM), row(D_D), row(kvw), row(kvw), buf, buf, row(D_D)],
        out_specs=row(D_D),
        out_shape=jax.ShapeDtypeStruct(q.shape, F32),
        compiler_params=_params(("parallel",)),
        name="swa_sample",
    )(sinks, q, kn, vn, kt, vt, g)


FOX_PAGES_PER_STEP = 8


def kernel(x_prompt, x_sample, cache_pool, cache_k, cache_v, cache_logf, state_hgrn, cache_win_k,
           cache_win_v, page_table, meta_tokens, w_in_e, b_f, w_pool, pool_scale, w_out_e, ln_g_e,
           ln_b_e, w_in_o, hgrn_gamma, hgrn_norm_g, sinks, w_out_o, ln_g_o, ln_b_o):
    assert w_in_e.shape[0] == 1 and w_in_o.shape[0] == 1 and hgrn_gamma.shape[0] == DEPTH
    nb, sp, _ = x_prompt.shape
    ns, ts, _ = x_sample.shape
    assert ts == 8 and sp % BLOCK == 0
    lp = sp + BLOCK
    n_pool = cache_k.shape[1]
    wb = cache_win_k.shape[2]
    assert wb == WINDOW and cache_k.shape[2] == LANES
    ptm = lp // 4
    stm = min(256, ns * ts)
    assert lp % (4 * SUBLANES) == 0 and (ns * ts) % stm == 0

    hp = jnp.concatenate([jnp.zeros((nb, PAD, D_MODEL), F32),
                          jnp.broadcast_to(meta_tokens, (nb, N_META, D_MODEL)), x_prompt], axis=1)
    hp = hp.reshape(nb * lp, D_MODEL)
    hs = x_sample.reshape(ns * ts, D_MODEL)

    we = w_in_e[0]
    w6 = we[:, :6 * 512].astype(BF16)
    wf = jnp.pad(we[:, 6 * 512:], ((0, 0), (0, LANES - H_B))).astype(BF16)
    bf = jnp.pad(b_f[0][None, :], ((0, 0), (0, LANES - H_B)))
    wa = w_out_e[0, :D_A].astype(BF16)
    wbm = w_out_e[0, D_A:].astype(BF16)
    sc = pool_scale[0][None, :]
    lng, lnb = ln_g_e[0][None, :], ln_b_e[0][None, :]

    u_p, ga_p, q_p, k_p, v_p, gb_p, lf_p = _proj_even(hp, w6, wf, bf, nb=nb, seq=lp, tm=ptm, first_valid=PAD)
    ya_p = _pool(u_p, ga_p, w_pool[0], sc, nb=nb, seq=lp, tr=ptm, first_valid=PAD)
    aq, ak = _fcum(lf_p, nb=nb, seq=lp)
    yb_p = _fox_prompt(q_p, k_p, v_p, aq, ak, gb_p, nb=nb, seq=lp, first_valid=PAD)
    hp1 = _out_ln(ya_p, yb_p, hp, wa, wbm, lng, lnb, nb=nb, seq=lp, tm=ptm, first_valid=PAD)

    u_s, ga_s, q_s, k_s, v_s, gb_s, lf_s = _proj_even(hs, w6, wf, bf, nb=1, seq=ns * ts, tm=stm, first_valid=0)
    hist = POOL_BUF + ts + 1
    u3 = u_s.reshape(ns, ts, D_A)
    u_ext = jnp.concatenate([cache_pool[0], u3, jnp.zeros((ns, 1, D_A), F32)], axis=1)
    g_ext = jnp.concatenate([jnp.zeros((ns, POOL_BUF, D_A), F32), ga_s.reshape(ns, ts, D_A),
                             jnp.zeros((ns, 1, D_A), F32)], axis=1)
    ya_s = _pool(u_ext.reshape(ns * hist, D_A), g_ext.reshape(ns * hist, D_A), w_pool[0], sc,
                 nb=ns, seq=hist, tr=hist, first_valid=0)
    ya_s = ya_s.reshape(ns, hist, D_A)[:, POOL_BUF:POOL_BUF + ts].reshape(ns * ts, D_A)
    kt = jnp.transpose(cache_k, (0, 1, 3, 4, 2)).reshape(1, n_pool, D_B, LANES)
    vt = jnp.transpose(cache_v, (0, 1, 3, 4, 2)).reshape(1, n_pool, D_B, LANES)
    lfc = jnp.transpose(cache_logf, (0, 1, 3, 2))
    lft = jnp.transpose(lf_s[:, :H_B].reshape(ns, ts, H_B), (0, 2, 1))
    lft = jnp.pad(lft, ((0, 0), (0, 0), (0, LANES - ts)))
    yb_s = _fox_sample(page_table, q_s, k_s, v_s, lft, gb_s, kt, vt, lfc, gp=FOX_PAGES_PER_STEP)
    hs1 = _out_ln(ya_s, yb_s, hs, wa, wbm, lng, lnb, nb=1, seq=ns * ts, tm=stm, first_valid=0)

    new_pool_p = u_p.reshape(nb, lp, D_A)[:, -POOL_BUF:][None]
    new_pool_s = u_ext[:, ts:ts + POOL_BUF][None]
    new_k_p = k_p.reshape(nb, lp, H_B, DH_B)[:, PAD:][None]
    new_v_p = v_p.reshape(nb, lp, H_B, DH_B)[:, PAD:][None]
    new_lf_p = lf_p.reshape(nb, lp, LANES)[:, PAD:, :H_B][None]
    new_k_s = k_s.reshape(ns, ts, H_B, DH_B)[None]
    new_v_s = v_s.reshape(ns, ts, H_B, DH_B)[None]
    new_lf_s = lf_s[:, :H_B].reshape(ns, ts, H_B)[None]

    wo = w_in_o[0].astype(BF16)
    wc = w_out_o[0, :D_C].astype(BF16)
    wd = w_out_o[0, D_C:].astype(BF16)
    ng = hgrn_norm_g[0][None, :]
    sk = sinks[0]
    lng, lnb = ln_g_o[0][None, :], ln_b_o[0][None, :]

    qc, kc, lfh, vc, gc, qd, kd, vd, gd = _proj_odd(hp1, wo, hgrn_gamma, nb=nb, seq=lp, tm=ptm, first_valid=PAD)
    yc_p, hg_p = _hgrn(qc, kc, lfh, vc, gc, ng, None, nb=nb, seq=lp, c=HGRN_CHUNK)
    yd_p = _swa_prompt(sk, qd, kd, vd, gd, nb=nb, seq=lp, first_valid=PAD)
    hp2 = _out_ln(yc_p, yd_p, hp1, wc, wd, lng, lnb, nb=nb, seq=lp, tm=ptm, first_valid=PAD)

    qc, kc, lfh, vc, gc, qd, kd_s, vd_s, gd = _proj_odd(hs1, wo, hgrn_gamma, nb=1, seq=ns * ts, tm=stm, first_valid=0)
    yc_s, hg_s = _hgrn(qc, kc, lfh, vc, gc, ng, state_hgrn[0], nb=ns, seq=ts, c=ts)
    kvw = KV_D * DH_D
    wkt = jnp.transpose(cache_win_k, (0, 1, 3, 4, 2)).reshape(1, ns, kvw, wb)
    wvt = jnp.transpose(cache_win_v, (0, 1, 3, 4, 2)).reshape(1, ns, kvw, wb)
    yd_s = _swa_sample(sk, qd, kd_s, vd_s, wkt, wvt, gd)
    hs2 = _out_ln(yc_s, yd_s, hs1, wc, wd, lng, lnb, nb=1, seq=ns * ts, tm=stm, first_valid=0)

    kd4 = kd.reshape(nb, lp, KV_D, DH_D)
    vd4 = vd.reshape(nb, lp, KV_D, DH_D)
    new_wk_p = kd4[:, -wb:][None]
    new_wv_p = vd4[:, -wb:][None]
    new_wk_s = jnp.concatenate([cache_win_k[0], kd_s.reshape(ns, ts, KV_D, DH_D)], axis=1)[:, -wb:][None]
    new_wv_s = jnp.concatenate([cache_win_v[0], vd_s.reshape(ns, ts, KV_D, DH_D)], axis=1)[:, -wb:][None]

    y_prompt = hp2.reshape(nb, lp, D_MODEL)[:, BLOCK:]
    y_sample = hs2.reshape(ns, ts, D_MODEL)
    return (y_prompt, y_sample, new_pool_p, new_pool_s, new_k_p, new_v_p, new_lf_p, new_k_s, new_v_s,
            new_lf_s, hg_p[None], hg_s[None], new_wk_p, new_wv_p, new_wk_s, new_wv_s)
```

```python
import functools

import numpy as np
import jax
import jax.numpy as jnp
from jax import lax
from jax.experimental import pallas as pl
from jax.experimental.pallas import tpu as pltpu

F32 = jnp.float32
BF16 = jnp.bfloat16

D_MODEL = 1024
DEPTH = 2
N_META = 16
BLOCK = 128
PAD = BLOCK - N_META
D_A = D_MODEL // 2
POOL_WINDOWS = (2, 4, 8, 16)
GC = D_A // len(POOL_WINDOWS)
W_MAX = 16
POOL_BUF = W_MAX - 1
H_B = D_MODEL // 128
DH_B = 64
D_B = H_B * DH_B
H_C = D_MODEL // 256
DK_C = 128
DV_C = 128
HK_C = H_C * DK_C
D_C = H_C * DV_C
HGRN_CHUNK = 64
H_D = D_MODEL // 128
KV_D = H_D // 4
G_D = H_D // KV_D
DH_D = 64
D_D = H_D * DH_D
WINDOW = 128
ALPHA = (2 * DEPTH) ** 0.25
LN_EPS = 1e-5
NEG = -1e30

LANES = 128
SUBLANES = 8
VMEM_LIMIT = 48 * 1024 * 1024

FOX_PAGES_PER_STEP = 8
FOX_SLOTS = 3
SEQS_PER_STEP = 8

_NT = (((1,), (1,)), ((), ()))


def _silu(x):
    return x * (1.0 / (1.0 + jnp.exp(-x)))


def _split3(x):
    hi = x.astype(BF16).astype(F32)
    r = x - hi
    mid = r.astype(BF16).astype(F32)
    return hi, mid, r - mid


def _dot(a, b):
    return jnp.dot(a, b, preferred_element_type=F32)


def _dot_nt(a, b):
    return lax.dot_general(a, b, _NT, preferred_element_type=F32)


def _dot3(m, x):
    hi, mid, lo = _split3(x)
    return _dot(m, hi) + _dot(m, mid) + _dot(m, lo)


def _dot3_rhs(x, m):
    hi, mid, lo = _split3(x)
    return _dot(hi, m) + _dot(mid, m) + _dot(lo, m)


def _params(sem):
    return pltpu.CompilerParams(dimension_semantics=sem, vmem_limit_bytes=VMEM_LIMIT)


def _proj_even_kernel(x_ref, w_ref, wf_ref, bf_ref, u_ref, ga_ref, q_ref, k_ref, v_ref, gb_ref,
                      lf_ref, *, tm, first_valid):
    xb = x_ref[...].astype(BF16)
    for i, o_ref in enumerate((u_ref, ga_ref, q_ref, k_ref, v_ref, gb_ref)):
        o_ref[...] = _dot(xb, w_ref[:, i * 512:(i + 1) * 512])
    z = _dot(xb, wf_ref[...]) + bf_ref[...]
    lf = jnp.minimum(z, 0.0) - jnp.log1p(jnp.exp(-jnp.abs(z)))
    pos = pl.program_id(1) * tm + lax.broadcasted_iota(jnp.int32, lf.shape, 0)
    lane = lax.broadcasted_iota(jnp.int32, lf.shape, 1)
    lf_ref[...] = jnp.where((pos >= first_valid) & (lane < H_B), lf, 0.0)


def _proj_even(x, w6, wf, bf, *, nb, seq, tm, first_valid):
    nt = seq // tm
    rows = nb * seq
    row_spec = lambda c: pl.BlockSpec((tm, c), lambda b, j: (b * nt + j, 0))
    full = lambda a: pl.BlockSpec(a.shape, lambda b, j: (0,) * a.ndim)
    outs = [jax.ShapeDtypeStruct((rows, 512), F32)] * 6 + [jax.ShapeDtypeStruct((rows, LANES), F32)]
    return pl.pallas_call(
        functools.partial(_proj_even_kernel, tm=tm, first_valid=first_valid),
        grid=(nb, nt),
        in_specs=[row_spec(D_MODEL), full(w6), full(wf), full(bf)],
        out_specs=[row_spec(512)] * 6 + [row_spec(LANES)],
        out_shape=outs,
        compiler_params=_params(("parallel", "parallel")),
        name="proj_even",
    )(x, w6, wf, bf)


def _proj_odd_kernel(x_ref, w_ref, gam_ref, qc_ref, kc_ref, lf_ref, vc_ref, gc_ref, qd_ref, kd_ref,
                     vd_ref, gd_ref, *, tm, first_valid):
    xb = x_ref[...].astype(BF16)
    g0 = gam_ref[0:1, :]
    g1 = gam_ref[1:2, :]
    mx = jnp.maximum(g0, g1)
    e0 = jnp.exp(g0 - mx)
    e1 = jnp.exp(g1 - mx)
    p0 = e0 / (e0 + e1)
    p1 = e1 / (e0 + e1)
    lb = (p0 + p1) - p0
    pos = pl.program_id(1) * tm + lax.broadcasted_iota(jnp.int32, (tm, 1), 0)
    valid = pos >= first_valid
    qc_ref[...] = _dot(xb, w_ref[:, 0:512])
    fc = _dot(xb, w_ref[:, 512:1024])
    f = lb + (1.0 - lb) * (1.0 / (1.0 + jnp.exp(-fc)))
    kc_ref[...] = jnp.where(valid, 1.0 - f, 0.0)
    lf_ref[...] = jnp.where(valid, jnp.log(f), 0.0)
    vc_ref[...] = _dot(xb, w_ref[:, 1024:1536])
    gc_ref[...] = _dot(xb, w_ref[:, 1536:2048])
    qd_ref[...] = _dot(xb, w_ref[:, 2048:2560])
    kd_ref[...] = _dot(xb, w_ref[:, 2560:2688])
    vd_ref[...] = _dot(xb, w_ref[:, 2688:2816])
    gd_ref[...] = _dot(xb, w_ref[:, 2816:3328])


def _proj_odd(x, w, gam, *, nb, seq, tm, first_valid):
    nt = seq // tm
    rows = nb * seq
    row_spec = lambda c: pl.BlockSpec((tm, c), lambda b, j: (b * nt + j, 0))
    full = lambda a: pl.BlockSpec(a.shape, lambda b, j: (0,) * a.ndim)
    widths = (512, 512, 512, 512, 512, 512, 128, 128, 512)
    return pl.pallas_call(
        functools.partial(_proj_odd_kernel, tm=tm, first_valid=first_valid),
        grid=(nb, nt),
        in_specs=[row_spec(D_MODEL), full(w), full(gam)],
        out_specs=[row_spec(c) for c in widths],
        out_shape=[jax.ShapeDtypeStruct((rows, c), F32) for c in widths],
        compiler_params=_params(("parallel", "parallel")),
        name="proj_odd",
    )(x, w, gam)


def _out_ln_kernel(a_ref, b_ref, x_ref, wa_ref, wb_ref, g_ref, bb_ref, o_ref, *, tm, first_valid):
    acc = _dot(a_ref[...].astype(BF16), wa_ref[...]) + _dot(b_ref[...].astype(BF16), wb_ref[...])
    y = ALPHA * x_ref[...] + acc
    mu = jnp.mean(y, axis=-1, keepdims=True)
    yc = y - mu
    var = jnp.mean(yc * yc, axis=-1, keepdims=True)
    o = yc * lax.rsqrt(var + LN_EPS) * g_ref[...] + bb_ref[...]
    pos = pl.program_id(1) * tm + lax.broadcasted_iota(jnp.int32, (tm, 1), 0)
    o_ref[...] = jnp.where(pos >= first_valid, o, 0.0)


def _out_ln(a, b, x, wa, wb, g, bb, *, nb, seq, tm, first_valid):
    nt = seq // tm
    row_spec = lambda c: pl.BlockSpec((tm, c), lambda i, j: (i * nt + j, 0))
    full = lambda t: pl.BlockSpec(t.shape, lambda i, j: (0,) * t.ndim)
    return pl.pallas_call(
        functools.partial(_out_ln_kernel, tm=tm, first_valid=first_valid),
        grid=(nb, nt),
        in_specs=[row_spec(512), row_spec(512), row_spec(D_MODEL), full(wa), full(wb), full(g), full(bb)],
        out_specs=row_spec(D_MODEL),
        out_shape=jax.ShapeDtypeStruct(x.shape, F32),
        compiler_params=_params(("parallel", "parallel")),
        name="out_ln",
    )(a, b, x, wa, wb, g, bb)


def _pool_kernel(u_ref, g_ref, wp_ref, sc_ref, y_ref, buf_ref, *, tr, nsub, first_valid):
    j = pl.program_id(1)
    pos = lax.broadcasted_iota(jnp.int32, (tr, 1), 0)
    if nsub == 1:
        pos = pos + j * tr
    for q in range(nsub):
        rs = slice(q * tr, (q + 1) * tr)
        if nsub == 1:
            @pl.when(j == 0)
            def _():
                buf_ref[0:W_MAX, :] = jnp.zeros((W_MAX, D_A), F32)

            @pl.when(j > 0)
            def _():
                buf_ref[0:W_MAX, :] = buf_ref[tr:tr + W_MAX, :]
        else:
            buf_ref[0:W_MAX, :] = jnp.zeros((W_MAX, D_A), F32)
        uf = jnp.where(pos >= first_valid, u_ref[rs, :], 0.0)
        buf_ref[W_MAX:W_MAX + tr, :] = uf
        for g, w in enumerate(POOL_WINDOWS):
            cs = slice(g * GC, (g + 1) * GC)
            s = buf_ref[W_MAX:W_MAX + tr, cs]
            for i in range(1, w):
                s = s + buf_ref[W_MAX - i:W_MAX - i + tr, cs]
            cnt = jnp.clip(pos - first_valid + 1, 1, w).astype(F32)
            d = s / cnt - uf[:, cs]
            y = _dot(d, wp_ref[g]) * sc_ref[:, cs]
            y_ref[rs, cs] = y * _silu(g_ref[rs, cs])


def _pool(u, g, wp, sc, *, nb, seq, tr, first_valid, nsub=1):
    nt = seq // tr
    assert nsub == 1 or (nt == 1 and nb % nsub == 0)
    nb = nb // nsub
    row_spec = pl.BlockSpec((nsub * tr, D_A), lambda b, j: (b * nt + j, 0))
    return pl.pallas_call(
        functools.partial(_pool_kernel, tr=tr, nsub=nsub, first_valid=first_valid),
        grid=(nb, nt),
        in_specs=[row_spec, row_spec,
                  pl.BlockSpec(wp.shape, lambda b, j: (0, 0, 0)),
                  pl.BlockSpec(sc.shape, lambda b, j: (0, 0))],
        out_specs=row_spec,
        out_shape=jax.ShapeDtypeStruct(u.shape, F32),
        scratch_shapes=[pltpu.VMEM((tr + W_MAX, D_A), F32)],
        compiler_params=_params(("parallel", "arbitrary")),
        name="pool",
    )(u, g, wp, sc)


def _fcum_kernel(lf_ref, aq_ref, ak_ref, *, nt):
    t = lax.broadcasted_iota(jnp.int32, (BLOCK, BLOCK), 0)
    lane = lax.broadcasted_iota(jnp.int32, (BLOCK, BLOCK), 1)
    tri = jnp.where(t >= lane, 1.0, 0.0).astype(F32)
    carry = jnp.zeros((1, LANES), F32)
    for j in range(nt):
        rs = slice(j * BLOCK, (j + 1) * BLOCK)
        f = _dot3(tri, lf_ref[rs, :]) + carry
        carry = f[BLOCK - 1:BLOCK, :]
        fh, fm, fl = _split3(f)
        f3 = fh + pltpu.roll(fm, H_B, 1) + pltpu.roll(fl, 2 * H_B, 1)
        aq_ref[rs, :] = jnp.where((lane >= 3 * H_B) & (lane < 6 * H_B), 1.0, f3)
        ak_ref[rs, :] = jnp.where(lane < 3 * H_B, 1.0, -pltpu.roll(f3, 3 * H_B, 1))


def _fcum(lf, *, nb, seq):
    spec = pl.BlockSpec((seq, LANES), lambda b: (b, 0))
    sds = jax.ShapeDtypeStruct(lf.shape, F32)
    return pl.pallas_call(
        functools.partial(_fcum_kernel, nt=seq // BLOCK),
        grid=(nb,),
        in_specs=[spec],
        out_specs=[spec, spec],
        out_shape=[sds, sds],
        compiler_params=_params(("parallel",)),
        name="fcum",
    )(lf)


def _fox_prompt_kernel(q_ref, k_ref, v_ref, aq_ref, ak_ref, gb_ref, y_ref, *, first_valid, klens):
    i = pl.program_id(1)
    lane = lax.broadcasted_iota(jnp.int32, (BLOCK, LANES), 1)
    low = lane < DH_B
    row = lax.broadcasted_iota(jnp.int32, (2 * BLOCK, 1), 0)
    qpos = i * BLOCK + (row & (BLOCK - 1))

    def attend(lk):
        kpos = lax.broadcasted_iota(jnp.int32, (1, lk), 1)
        mask = (kpos <= qpos) & (kpos >= first_valid)
        aq = aq_ref[...]
        ak = ak_ref[0:lk, :]
        for j in range(H_B // 2):
            cs = slice(j * LANES, (j + 1) * LANES)
            qp = q_ref[:, cs] * (DH_B ** -0.5)
            a0 = jnp.where(((lane & (H_B - 1)) == 2 * j) & (lane < 6 * H_B), aq, 0.0)
            a1 = jnp.where(((lane & (H_B - 1)) == 2 * j + 1) & (lane < 6 * H_B), aq, 0.0)
            qs = jnp.concatenate([
                jnp.concatenate([jnp.where(low, qp, 0.0), a0], axis=1),
                jnp.concatenate([jnp.where(low, 0.0, qp), a1], axis=1)], axis=0)
            ks = jnp.concatenate([k_ref[0:lk, cs], ak], axis=1)
            s = jnp.where(mask, _dot_nt(qs, ks), NEG)
            m = jnp.max(s, axis=-1, keepdims=True)
            p = jnp.exp(s - m)
            l = jnp.sum(p, axis=-1, keepdims=True)
            r = _dot(p, v_ref[0:lk, cs]) / l
            o = jnp.where(low, r[:BLOCK], r[BLOCK:])
            y_ref[:, cs] = o * _silu(gb_ref[:, cs])

    lo = 0
    for hi in klens:
        @pl.when((i >= lo) & (i < hi))
        def _(hi=hi):
            attend(hi * BLOCK)
        lo = hi


def _fox_prompt(q, k, v, aq, ak, gb, *, nb, seq, first_valid):
    nt = seq // BLOCK
    nvar = min(4, nt)
    klens = tuple(-(-nt * (x + 1) // nvar) for x in range(nvar))
    blk = lambda c: pl.BlockSpec((BLOCK, c), lambda b, i: (b * nt + i, 0))
    whole = lambda c: pl.BlockSpec((seq, c), lambda b, i: (b, 0))
    return pl.pallas_call(
        functools.partial(_fox_prompt_kernel, first_valid=first_valid, klens=klens),
        grid=(nb, nt),
        in_specs=[blk(D_B), whole(D_B), whole(D_B), blk(LANES), whole(LANES), blk(D_B)],
        out_specs=blk(D_B),
        out_shape=jax.ShapeDtypeStruct(q.shape, F32),
        compiler_params=_params(("parallel", "parallel")),
        name="fox_prompt",
    )(q, k, v, aq, ak, gb)


def _fox_page_copies(pt_ref, kt_hbm, vt_hbm, lf_hbm, kbuf, vbuf, lbuf, sem, seq, step, slot, *, gp, npages):
    copies = []
    for g in range(gp):
        page = pt_ref[seq, npages - (step + 1) * gp + g]
        copies.append(pltpu.make_async_copy(kt_hbm.at[0, page], kbuf.at[slot, g], sem.at[slot]))
        copies.append(pltpu.make_async_copy(vt_hbm.at[0, page], vbuf.at[slot, g], sem.at[slot]))
        copies.append(pltpu.make_async_copy(lf_hbm.at[0, page], lbuf.at[slot, g], sem.at[slot]))
    return copies


def _fox_sample_kernel(pt_ref, q_ref, kn_ref, vn_ref, lft_ref, gb_ref, ut_ref, kt_hbm, vt_hbm, lf_hbm,
                       y_ref, qbd_ref, m_ref, l_ref, acc_ref, suf_ref, kbuf, vbuf, lbuf, sem, *, gp, npages):
    seq = pl.program_id(0)
    step = pl.program_id(1)
    nseq = pl.num_programs(0)
    steps = npages // gp
    lin = seq * steps + step
    copies = functools.partial(_fox_page_copies, pt_ref, kt_hbm, vt_hbm, lf_hbm, kbuf, vbuf, lbuf, sem,
                               gp=gp, npages=npages)

    def start(cs):
        for idx, c in enumerate(cs):
            c.start(priority=idx % 2)

    @pl.when(lin == 0)
    def _():
        for d in range(FOX_SLOTS - 1):
            start(copies(0, d, d))

    ahead = step + (FOX_SLOTS - 1)
    wrap = (ahead >= steps).astype(jnp.int32)
    seq_a = seq + wrap
    step_a = ahead - wrap * steps

    @pl.when(seq_a < nseq)
    def _():
        start(copies(seq_a, step_a, lax.rem(lin + (FOX_SLOTS - 1), FOX_SLOTS)))

    slot = lax.rem(lin, FOX_SLOTS)
    for c in copies(seq, step, slot):
        c.wait()

    lane512 = lax.broadcasted_iota(jnp.int32, (SUBLANES, D_B), 1)
    nrow = H_B * SUBLANES

    @pl.when(step == 0)
    def _():
        qs = q_ref[...] * (DH_B ** -0.5)
        for h in range(H_B):
            qbd_ref[h * 8:(h + 1) * 8, :] = jnp.where((lane512 >> 6) == h, qs, 0.0)
        lft = lft_ref[0]
        lane = lax.broadcasted_iota(jnp.int32, (H_B, LANES), 1)
        gcum = jnp.zeros((H_B, LANES), F32)
        for u in range(SUBLANES):
            gcum = gcum + jnp.where(lane >= u, jnp.broadcast_to(lft[:, u:u + 1], (H_B, LANES)), 0.0)
        kpad = jnp.concatenate([kn_ref[...], jnp.zeros((LANES - 8, D_B), F32)], axis=0)
        vpad = jnp.concatenate([vn_ref[...], jnp.zeros((LANES - 8, D_B), F32)], axis=0)
        s = _dot_nt(qbd_ref[...], kpad)
        bias = jnp.concatenate(
            [jnp.broadcast_to(gcum[h:h + 1, :], (8, LANES)) for h in range(H_B)], axis=0)
        t = lax.broadcasted_iota(jnp.int32, (nrow, LANES), 0) & 7
        sl = lax.broadcasted_iota(jnp.int32, (nrow, LANES), 1)
        s = jnp.where(sl <= t, s - bias, NEG)
        m = jnp.max(s, axis=-1, keepdims=True)
        p = jnp.exp(s - m)
        m_ref[...] = m
        l_ref[...] = jnp.sum(p, axis=-1, keepdims=True)
        acc_ref[...] = _dot(p, vpad)
        suf_ref[...] = jnp.zeros(suf_ref.shape, F32)

    lfs = jnp.concatenate([lbuf[slot, g] for g in range(gp)], axis=0)
    wt = _dot3_rhs(lfs, ut_ref[...])
    suf = suf_ref[...]
    biases = [None] * gp
    for g in range(gp - 1, -1, -1):
        biases[g] = wt[g * 8:(g + 1) * 8, :LANES] + suf
        suf = suf + wt[g * 8:(g + 1) * 8, LANES:]
    suf_ref[...] = suf

    qbd = qbd_ref[...]
    ss = []
    for g in range(gp):
        sg = _dot(qbd, kbuf[slot, g])
        bg = jnp.concatenate(
            [jnp.broadcast_to(biases[g][h:h + 1, :], (8, LANES)) for h in range(H_B)], axis=0)
        ss.append(sg + bg)
    m_prev = m_ref[...]
    m_cur = ss[0]
    for g in range(1, gp):
        m_cur = jnp.maximum(m_cur, ss[g])
    m_new = jnp.maximum(m_prev, jnp.max(m_cur, axis=-1, keepdims=True))
    alpha = jnp.exp(m_prev - m_new)
    psum = None
    pv = None
    for g in range(gp):
        p = jnp.exp(ss[g] - m_new)
        psum = p if psum is None else psum + p
        d = _dot_nt(p, vbuf[slot, g])
        pv = d if pv is None else pv + d
    m_ref[...] = m_new
    l_ref[...] = alpha * l_ref[...] + jnp.sum(psum, axis=-1, keepdims=True)
    acc_ref[...] = alpha * acc_ref[...] + pv

    @pl.when(step == steps - 1)
    def _():
        o = acc_ref[...] / l_ref[...]
        out = jnp.zeros((SUBLANES, D_B), F32)
        for h in range(H_B):
            out = out + jnp.where((lane512 >> 6) == h, o[h * 8:(h + 1) * 8, :], 0.0)
        y_ref[...] = out * _silu(gb_ref[...])


def _fox_sample(page_table, q, kn, vn, lft, gb, kt, vt, lfc, *, gp):
    n, npages = page_table.shape
    steps = npages // gp
    assert npages % gp == 0 and steps >= FOX_SLOTS - 1
    s_i = np.arange(LANES)
    ut = np.concatenate([(s_i[:, None] > s_i[None, :]).astype(np.float32),
                         np.ones((LANES, LANES), np.float32)], axis=1)
    row = lambda c: pl.BlockSpec((8, c), lambda i, s, pt: (i, 0))
    hbm = pl.BlockSpec(memory_space=pl.ANY)
    in_specs = [row(D_B), row(D_B), row(D_B),
                pl.BlockSpec((1, H_B, LANES), lambda i, s, pt: (i, 0, 0)),
                row(D_B),
                pl.BlockSpec(ut.shape, lambda i, s, pt: (0, 0)),
                hbm, hbm, hbm]
    nrow = H_B * SUBLANES
    grid_spec = pltpu.PrefetchScalarGridSpec(
        num_scalar_prefetch=1,
        grid=(n, steps),
        in_specs=in_specs,
        out_specs=pl.BlockSpec((8, D_B), lambda i, s, pt: (i, 0)),
        scratch_shapes=[pltpu.VMEM((nrow, D_B), F32), pltpu.VMEM((nrow, 1), F32),
                        pltpu.VMEM((nrow, 1), F32), pltpu.VMEM((nrow, D_B), F32),
                        pltpu.VMEM((H_B, LANES), F32),
                        pltpu.VMEM((FOX_SLOTS, gp, D_B, LANES), F32),
                        pltpu.VMEM((FOX_SLOTS, gp, D_B, LANES), F32),
                        pltpu.VMEM((FOX_SLOTS, gp, H_B, LANES), F32),
                        pltpu.SemaphoreType.DMA((FOX_SLOTS,))])
    return pl.pallas_call(
        functools.partial(_fox_sample_kernel, gp=gp, npages=npages),
        grid_spec=grid_spec,
        out_shape=jax.ShapeDtypeStruct(q.shape, F32),
        compiler_params=_params(("arbitrary", "arbitrary")),
        name="fox_sample",
    )(page_table, q, kn, vn, lft, gb, jnp.asarray(ut), kt, vt, lfc)


def _hgrn_levels(c):
    out = []
    m = c // 2
    while m >= 1:
        out.append(m)
        m //= 2
    return tuple(out)


def _hgrn_mstack(c):
    t = np.arange(c)[:, None]
    u = np.arange(c)[None, :]
    mats = [(u <= t)]
    for m in _hgrn_levels(c):
        r = (t // (2 * m)) * (2 * m) + m - 1
        second = (t % (2 * m)) >= m
        mats.append(np.where(second, (u > r) & (u <= t), (u > t) & (u <= r)))
    return np.concatenate(mats, axis=0).astype(np.float32)


def _hgrn_kernel(*refs, c, nsub, has_s0):
    if has_s0:
        q_ref, k_ref, lf_ref, v_ref, g_ref, m_ref, ng_ref, s0_ref, y_ref, so_ref, st_ref = refs
    else:
        q_ref, k_ref, lf_ref, v_ref, g_ref, m_ref, ng_ref, y_ref, so_ref, st_ref = refs
    ci = pl.program_id(1)
    levels = _hgrn_levels(c)
    units = [(q, h) for q in range(nsub) for h in range(H_C)]

    @pl.when(ci == 0)
    def _():
        for q, h in units:
            st_ref[q * H_C + h] = s0_ref[q, h].T if has_s0 else jnp.zeros((DV_C, DK_C), F32)

    t_col = lax.broadcasted_iota(jnp.int32, (c, 1), 0)
    t_row = lax.broadcasted_iota(jnp.int32, (c, c), 0)
    s_col = lax.broadcasted_iota(jnp.int32, (c, c), 1)
    rpad = max(LANES - c, 0)
    mstack = m_ref[...]
    sub = {}
    for q in range(nsub):
        rs = slice(q * c, (q + 1) * c)
        e = _dot3(mstack, lf_ref[rs, :])
        b = e[0:c]
        qv, kv = q_ref[rs, :], k_ref[rs, :]
        bl = b[c - 1:c, :]
        qx, kx = [], []
        for li, m in enumerate(levels):
            x = jnp.exp(e[(li + 1) * c:(li + 2) * c])
            second = (t_col & (2 * m - 1)) >= m
            qx.append(jnp.where(second, qv * x, 0.0))
            kx.append(jnp.where(second, 0.0, kv * x))
        sub[q] = dict(rs=rs, qb=qv * jnp.exp(b), qk=qv * kv, kp=kv * jnp.exp(bl - b), dec=jnp.exp(bl),
                      qx=qx, kx=kx, v=v_ref[rs, :])
    o_inter, a_mat = {}, {}
    for q, h in units:
        cs = slice(h * DK_C, (h + 1) * DK_C)
        d = sub[q]
        o_inter[q, h] = _dot_nt(d["qb"][:, cs], st_ref[q * H_C + h])
        a = jnp.where(t_row == s_col, jnp.sum(d["qk"][:, cs], axis=-1, keepdims=True), 0.0)
        for li, m in enumerate(levels):
            sh = (2 * m).bit_length() - 1
            am = _dot_nt(d["qx"][li][:, cs], d["kx"][li][:, cs])
            a = a + jnp.where((t_row >> sh) == (s_col >> sh), am, 0.0)
        a_mat[q, h] = a
    for q, h in units:
        cs = slice(h * DK_C, (h + 1) * DK_C)
        d = sub[q]
        v = d["v"][:, cs]
        kp = d["kp"][:, cs]
        o = o_inter[q, h] + _dot(a_mat[q, h], v)
        if rpad:
            v = jnp.concatenate([v, jnp.zeros((rpad, DV_C), F32)], axis=0)
            kp = jnp.concatenate([kp, jnp.zeros((rpad, DK_C), F32)], axis=0)
        st_ref[q * H_C + h] = st_ref[q * H_C + h] * d["dec"][:, cs] + _dot(v.T, kp)
        y = o * lax.rsqrt(jnp.mean(o * o, axis=-1, keepdims=True) + LN_EPS) * ng_ref[...]
        y_ref[d["rs"], cs] = y * _silu(g_ref[d["rs"], cs])

    @pl.when(ci == pl.num_programs(1) - 1)
    def _():
        for q, h in units:
            so_ref[q, h] = st_ref[q * H_C + h].T


def _hgrn(q, k, lf, v, g, ng, s0, *, nb, seq, c, nsub=1):
    nc = seq // c
    assert nsub == 1 or (nc == 1 and nb % nsub == 0)
    ms = jnp.asarray(_hgrn_mstack(c))
    blk = pl.BlockSpec((nsub * c, HK_C), lambda b, j: (b * nc + j, 0))
    st_spec = pl.BlockSpec((nsub, H_C, DK_C, DV_C), lambda b, j: (b, 0, 0, 0))
    in_specs = [blk, blk, blk, blk, blk,
                pl.BlockSpec(ms.shape, lambda b, j: (0, 0)),
                pl.BlockSpec(ng.shape, lambda b, j: (0, 0))]
    args = [q, k, lf, v, g, ms, ng]
    if s0 is not None:
        in_specs.append(st_spec)
        args.append(s0)
    return pl.pallas_call(
        functools.partial(_hgrn_kernel, c=c, nsub=nsub, has_s0=s0 is not None),
        grid=(nb // nsub, nc),
        in_specs=in_specs,
        out_specs=[blk, st_spec],
        out_shape=[jax.ShapeDtypeStruct(q.shape, F32),
                   jax.ShapeDtypeStruct((nb, H_C, DK_C, DV_C), F32)],
        scratch_shapes=[pltpu.VMEM((nsub * H_C, DV_C, DK_C), F32)],
        compiler_params=_params(("parallel", "arbitrary")),
        name="hgrn",
    )(*args)


def _swa_prompt_kernel(sk_ref, q_ref, kp_ref, kc_ref, vp_ref, vc_ref, g_ref, y_ref, *, first_valid):
    i = pl.program_id(1)
    lane = lax.broadcasted_iota(jnp.int32, (BLOCK, LANES), 1)
    low = lane < DH_D
    kk = jnp.concatenate([kp_ref[...], kc_ref[...]], axis=0)
    vv = jnp.concatenate([vp_ref[...], vc_ref[...]], axis=0)
    kk_sw = pltpu.roll(kk, DH_D, 1)
    vv_sw = pltpu.roll(vv, DH_D, 1)
    kj = lax.broadcasted_iota(jnp.int32, (BLOCK, 2 * BLOCK), 1)
    tq = lax.broadcasted_iota(jnp.int32, (BLOCK, 2 * BLOCK), 0)
    dist = tq + BLOCK - kj
    kpos = i * BLOCK - BLOCK + kj
    mask = (dist >= 0) & (dist <= WINDOW) & (kpos >= first_valid)
    distf = dist.astype(F32)
    ss = []
    for h in range(H_D):
        j, e, kvh = h // 2, h % 2, h // G_D
        qp = q_ref[:, j * LANES:(j + 1) * LANES] * (DH_D ** -0.5)
        qm = jnp.where(low, qp, 0.0) if e == 0 else jnp.where(low, 0.0, qp)
        s = _dot_nt(qm, kk if e == kvh else kk_sw) - (2.0 ** -(h + 1)) * distf
        ss.append(jnp.where(mask, s, NEG))
    ps, dens = [], []
    for h in range(H_D):
        sink = sk_ref[h]
        m = jnp.maximum(jnp.max(ss[h], axis=-1, keepdims=True), sink)
        p = jnp.exp(ss[h] - m)
        ps.append(p)
        dens.append(jnp.sum(p, axis=-1, keepdims=True) + jnp.exp(sink - m))
    outs = [_dot(ps[h], vv if h % 2 == h // G_D else vv_sw) / dens[h] for h in range(H_D)]
    for j in range(H_D // 2):
        cs = slice(j * LANES, (j + 1) * LANES)
        y_ref[:, cs] = jnp.where(low, outs[2 * j], outs[2 * j + 1]) * _silu(g_ref[:, cs])


def _swa_prompt(sinks, q, k, v, g, *, nb, seq, first_valid):
    nt = seq // BLOCK
    cur = lambda c: pl.BlockSpec((BLOCK, c), lambda b, i: (b * nt + i, 0))
    prev = lambda c: pl.BlockSpec((BLOCK, c), lambda b, i: (b * nt + jnp.maximum(i - 1, 0), 0))
    kvw = KV_D * DH_D
    return pl.pallas_call(
        functools.partial(_swa_prompt_kernel, first_valid=first_valid),
        grid=(nb, nt),
        in_specs=[pl.BlockSpec(memory_space=pltpu.SMEM),
                  cur(D_D), prev(kvw), cur(kvw), prev(kvw), cur(kvw), cur(D_D)],
        out_specs=cur(D_D),
        out_shape=jax.ShapeDtypeStruct(q.shape, F32),
        compiler_params=_params(("parallel", "parallel")),
        name="swa_prompt",
    )(sinks, q, k, k, v, v, g)


def _swa_sample_kernel(sk_ref, q_ref, kn_ref, vn_ref, kt_ref, vt_ref, g_ref, y_ref, *, nsub):
    nrow = H_D * SUBLANES
    lane8 = lax.broadcasted_iota(jnp.int32, (SUBLANES, LANES), 1)
    rows = lax.broadcasted_iota(jnp.int32, (nrow, 1), 0)
    hrow = rows >> 3
    t = rows & 7
    slope = jnp.zeros((nrow, 1), F32)
    sink = jnp.zeros((nrow, 1), F32)
    for h in range(H_D):
        slope = jnp.where(hrow == h, 2.0 ** -(h + 1), slope)
        sink = jnp.where(hrow == h, sk_ref[h], sink)
    kj = lax.broadcasted_iota(jnp.int32, (nrow, LANES), 1)
    d_old = t + WINDOW - kj
    d_new = t - kj
    for u in range(nsub):
        rs = slice(u * 8, (u + 1) * 8)
        blocks = []
        for h in range(H_D):
            j, e, kvh = h // 2, h % 2, h // G_D
            qp = q_ref[rs, j * LANES:(j + 1) * LANES] * (DH_D ** -0.5)
            src = qp if e == kvh else pltpu.roll(qp, DH_D, 1)
            blocks.append(jnp.where((lane8 >> 6) == kvh, src, 0.0))
        qbd = jnp.concatenate(blocks, axis=0)
        kpad = jnp.concatenate([kn_ref[rs, :], jnp.zeros((LANES - 8, LANES), F32)], axis=0)
        vpad = jnp.concatenate([vn_ref[rs, :], jnp.zeros((LANES - 8, LANES), F32)], axis=0)
        s_old = _dot(qbd, kt_ref[0, u]) - slope * d_old.astype(F32)
        s_old = jnp.where(d_old <= WINDOW, s_old, NEG)
        s_new = _dot_nt(qbd, kpad) - slope * d_new.astype(F32)
        s_new = jnp.where(d_new >= 0, s_new, NEG)
        m = jnp.maximum(jnp.maximum(jnp.max(s_old, axis=-1, keepdims=True),
                                    jnp.max(s_new, axis=-1, keepdims=True)), sink)
        p_old = jnp.exp(s_old - m)
        p_new = jnp.exp(s_new - m)
        den = (jnp.sum(p_old, axis=-1, keepdims=True) + jnp.sum(p_new, axis=-1, keepdims=True)
               + jnp.exp(sink - m))
        o = (_dot_nt(p_old, vt_ref[0, u]) + _dot(p_new, vpad)) / den
        o_sw = pltpu.roll(o, DH_D, 1)
        for j in range(H_D // 2):
            parts = []
            for e in range(2):
                h = 2 * j + e
                parts.append((o if e == h // G_D else o_sw)[h * 8:(h + 1) * 8, :])
            cs = slice(j * LANES, (j + 1) * LANES)
            y_ref[rs, cs] = jnp.where(lane8 < DH_D, parts[0], parts[1]) * _silu(g_ref[rs, cs])


def _swa_sample(sinks, q, kn, vn, kt, vt, g, *, nsub):
    n = q.shape[0] // 8
    assert n % nsub == 0
    row = lambda c: pl.BlockSpec((nsub * 8, c), lambda i: (i, 0))
    buf = pl.BlockSpec((1, nsub, LANES, LANES), lambda i: (0, i, 0, 0))
    kvw = KV_D * DH_D
    return pl.pallas_call(
        functools.partial(_swa_sample_kernel, nsub=nsub),
        grid=(n // nsub,),
        in_specs=[pl.BlockSpec(memory_space=pltpu.SMEM), row(D_D), row(kvw), row(kvw), buf, buf, row(D_D)],
        out_specs=row(D_D),
        out_shape=jax.ShapeDtypeStruct(q.shape, F32),
        compiler_params=_params(("parallel",)),
        name="swa_sample",
    )(sinks, q, kn, vn, kt, vt, g)


def kernel(x_prompt, x_sample, cache_pool, cache_k, cache_v, cache_logf, state_hgrn, cache_win_k,
           cache_win_v, page_table, meta_tokens, w_in_e, b_f, w_pool, pool_scale, w_out_e, ln_g_e,
           ln_b_e, w_in_o, hgrn_gamma, hgrn_norm_g, sinks, w_out_o, ln_g_o, ln_b_o):
    assert w_in_e.shape[0] == 1 and w_in_o.shape[0] == 1 and hgrn_gamma.shape[0] == DEPTH
    nb, sp, _ = x_prompt.shape
    ns, ts, _ = x_sample.shape
    assert ts == 8 and sp % BLOCK == 0
    lp = sp + BLOCK
    n_pool = cache_k.shape[1]
    wb = cache_win_k.shape[2]
    assert wb == WINDOW and cache_k.shape[2] == LANES
    ptm = lp // 4
    nsub = max(d for d in range(1, SEQS_PER_STEP + 1) if ns % d == 0)
    stm = min(256, ns * ts)
    assert lp % (4 * SUBLANES) == 0 and (ns * ts) % stm == 0

    hp = jnp.concatenate([jnp.zeros((nb, PAD, D_MODEL), F32),
                          jnp.broadcast_to(meta_tokens, (nb, N_META, D_MODEL)), x_prompt], axis=1)
    hp = hp.reshape(nb * lp, D_MODEL)
    hs = x_sample.reshape(ns * ts, D_MODEL)

    we = w_in_e[0]
    w6 = we[:, :6 * 512].astype(BF16)
    wf = jnp.pad(we[:, 6 * 512:], ((0, 0), (0, LANES - H_B))).astype(BF16)
    bf = jnp.pad(b_f[0][None, :], ((0, 0), (0, LANES - H_B)))
    wa = w_out_e[0, :D_A].astype(BF16)
    wbm = w_out_e[0, D_A:].astype(BF16)
    sc = pool_scale[0][None, :]
    lng, lnb = ln_g_e[0][None, :], ln_b_e[0][None, :]

    u_p, ga_p, q_p, k_p, v_p, gb_p, lf_p = _proj_even(hp, w6, wf, bf, nb=nb, seq=lp, tm=ptm, first_valid=PAD)
    ya_p = _pool(u_p, ga_p, w_pool[0], sc, nb=nb, seq=lp, tr=ptm, first_valid=PAD)
    aq, ak = _fcum(lf_p, nb=nb, seq=lp)
    yb_p = _fox_prompt(q_p, k_p, v_p, aq, ak, gb_p, nb=nb, seq=lp, first_valid=PAD)
    hp1 = _out_ln(ya_p, yb_p, hp, wa, wbm, lng, lnb, nb=nb, seq=lp, tm=ptm, first_valid=PAD)

    u_s, ga_s, q_s, k_s, v_s, gb_s, lf_s = _proj_even(hs, w6, wf, bf, nb=1, seq=ns * ts, tm=stm, first_valid=0)
    hist = POOL_BUF + ts + 1
    u3 = u_s.reshape(ns, ts, D_A)
    u_ext = jnp.concatenate([cache_pool[0], u3, jnp.zeros((ns, 1, D_A), F32)], axis=1)
    g_ext = jnp.concatenate([jnp.zeros((ns, POOL_BUF, D_A), F32), ga_s.reshape(ns, ts, D_A),
                             jnp.zeros((ns, 1, D_A), F32)], axis=1)
    ya_s = _pool(u_ext.reshape(ns * hist, D_A), g_ext.reshape(ns * hist, D_A), w_pool[0], sc,
                 nb=ns, seq=hist, tr=hist, first_valid=0, nsub=nsub)
    ya_s = ya_s.reshape(ns, hist, D_A)[:, POOL_BUF:POOL_BUF + ts].reshape(ns * ts, D_A)
    kt = jnp.transpose(cache_k, (0, 1, 3, 4, 2)).reshape(1, n_pool, D_B, LANES)
    vt = jnp.transpose(cache_v, (0, 1, 3, 4, 2)).reshape(1, n_pool, D_B, LANES)
    lfc = jnp.transpose(cache_logf, (0, 1, 3, 2))
    lft = jnp.transpose(lf_s[:, :H_B].reshape(ns, ts, H_B), (0, 2, 1))
    lft = jnp.pad(lft, ((0, 0), (0, 0), (0, LANES - ts)))
    yb_s = _fox_sample(page_table, q_s, k_s, v_s, lft, gb_s, kt, vt, lfc, gp=FOX_PAGES_PER_STEP)
    hs1 = _out_ln(ya_s, yb_s, hs, wa, wbm, lng, lnb, nb=1, seq=ns * ts, tm=stm, first_valid=0)

    new_pool_p = u_p.reshape(nb, lp, D_A)[:, -POOL_BUF:][None]
    new_pool_s = u_ext[:, ts:ts + POOL_BUF][None]
    new_k_p = k_p.reshape(nb, lp, H_B, DH_B)[:, PAD:][None]
    new_v_p = v_p.reshape(nb, lp, H_B, DH_B)[:, PAD:][None]
    new_lf_p = lf_p.reshape(nb, lp, LANES)[:, PAD:, :H_B][None]
    new_k_s = k_s.reshape(ns, ts, H_B, DH_B)[None]
    new_v_s = v_s.reshape(ns, ts, H_B, DH_B)[None]
    new_lf_s = lf_s[:, :H_B].reshape(ns, ts, H_B)[None]

    wo = w_in_o[0].astype(BF16)
    wc = w_out_o[0, :D_C].astype(BF16)
    wd = w_out_o[0, D_C:].astype(BF16)
    ng = hgrn_norm_g[0][None, :]
    sk = sinks[0]
    lng, lnb = ln_g_o[0][None, :], ln_b_o[0][None, :]

    qc, kc, lfh, vc, gc, qd, kd, vd, gd = _proj_odd(hp1, wo, hgrn_gamma, nb=nb, seq=lp, tm=ptm, first_valid=PAD)
    yc_p, hg_p = _hgrn(qc, kc, lfh, vc, gc, ng, None, nb=nb, seq=lp, c=HGRN_CHUNK)
    yd_p = _swa_prompt(sk, qd, kd, vd, gd, nb=nb, seq=lp, first_valid=PAD)
    hp2 = _out_ln(yc_p, yd_p, hp1, wc, wd, lng, lnb, nb=nb, seq=lp, tm=ptm, first_valid=PAD)

    qc, kc, lfh, vc, gc, qd, kd_s, vd_s, gd = _proj_odd(hs1, wo, hgrn_gamma, nb=1, seq=ns * ts, tm=stm, first_valid=0)
    yc_s, hg_s = _hgrn(qc, kc, lfh, vc, gc, ng, state_hgrn[0], nb=ns, seq=ts, c=ts, nsub=nsub)
    kvw = KV_D * DH_D
    wkt = jnp.transpose(cache_win_k, (0, 1, 3, 4, 2)).reshape(1, ns, kvw, wb)
    wvt = jnp.transpose(cache_win_v, (0, 1, 3, 4, 2)).reshape(1, ns, kvw, wb)
    yd_s = _swa_sample(sk, qd, kd_s, vd_s, wkt, wvt, gd, nsub=nsub)
    hs2 = _out_ln(yc_s, yd_s, hs1, wc, wd, lng, lnb, nb=1, seq=ns * ts, tm=stm, first_valid=0)

    kd4 = kd.reshape(nb, lp, KV_D, DH_D)
    vd4 = vd.reshape(nb, lp, KV_D, DH_D)
    new_wk_p = kd4[:, -wb:][None]
    new_wv_p = vd4[:, -wb:][None]
    new_wk_s = jnp.concatenate([cache_win_k[0], kd_s.reshape(ns, ts, KV_D, DH_D)], axis=1)[:, -wb:][None]
    new_wv_s = jnp.concatenate([cache_win_v[0], vd_s.reshape(ns, ts, KV_D, DH_D)], axis=1)[:, -wb:][None]

    y_prompt = hp2.reshape(nb, lp, D_MODEL)[:, BLOCK:]
    y_sample = hs2.reshape(ns, ts, D_MODEL)
    return (y_prompt, y_sample, new_pool_p, new_pool_s, new_k_p, new_v_p, new_lf_p, new_k_s, new_v_s,
            new_lf_s, hg_p[None], hg_s[None], new_wk_p, new_wv_p, new_wk_s, new_wv_s)
```

```python
import functools

import numpy as np
import jax
import jax.numpy as jnp
from jax import lax
from jax.experimental import pallas as pl
from jax.experimental.pallas import tpu as pltpu

F32 = jnp.float32
BF16 = jnp.bfloat16

D_MODEL = 1024
DEPTH = 2
N_META = 16
BLOCK = 128
PAD = BLOCK - N_META
D_A = D_MODEL // 2
POOL_WINDOWS = (2, 4, 8, 16)
GC = D_A // len(POOL_WINDOWS)
W_MAX = 16
POOL_BUF = W_MAX - 1
H_B = D_MODEL // 128
DH_B = 64
D_B = H_B * DH_B
H_C = D_MODEL // 256
DK_C = 128
DV_C = 128
HK_C = H_C * DK_C
D_C = H_C * DV_C
HGRN_CHUNK = 64
H_D = D_MODEL // 128
KV_D = H_D // 4
G_D = H_D // KV_D
DH_D = 64
D_D = H_D * DH_D
WINDOW = 128
ALPHA = (2 * DEPTH) ** 0.25
LN_EPS = 1e-5
NEG = -1e30

LANES = 128
SUBLANES = 8
VMEM_LIMIT = 48 * 1024 * 1024

FOX_PAGES_PER_STEP = 8
FOX_SLOTS = 3
FOX_KEY_VARIANTS = 8
SEQS_PER_STEP = 8

_NT = (((1,), (1,)), ((), ()))


def _silu(x):
    return x * (1.0 / (1.0 + jnp.exp(-x)))


def _split3(x):
    hi = x.astype(BF16).astype(F32)
    r = x - hi
    mid = r.astype(BF16).astype(F32)
    return hi, mid, r - mid


def _dot(a, b):
    return jnp.dot(a, b, preferred_element_type=F32)


def _dot_nt(a, b):
    return lax.dot_general(a, b, _NT, preferred_element_type=F32)


def _dot3(m, x):
    hi, mid, lo = _split3(x)
    return _dot(m, hi) + _dot(m, mid) + _dot(m, lo)


def _dot3_rhs(x, m):
    hi, mid, lo = _split3(x)
    return _dot(hi, m) + _dot(mid, m) + _dot(lo, m)


def _params(sem):
    return pltpu.CompilerParams(dimension_semantics=sem, vmem_limit_bytes=VMEM_LIMIT)


def _proj_even_kernel(x_ref, w_ref, wf_ref, bf_ref, u_ref, ga_ref, q_ref, k_ref, v_ref, gb_ref,
                      lf_ref, *, tm, first_valid):
    xb = x_ref[...].astype(BF16)
    for i, o_ref in enumerate((u_ref, ga_ref, q_ref, k_ref, v_ref, gb_ref)):
        o_ref[...] = _dot(xb, w_ref[:, i * 512:(i + 1) * 512])
    z = _dot(xb, wf_ref[...]) + bf_ref[...]
    lf = jnp.minimum(z, 0.0) - jnp.log1p(jnp.exp(-jnp.abs(z)))
    pos = pl.program_id(1) * tm + lax.broadcasted_iota(jnp.int32, lf.shape, 0)
    lane = lax.broadcasted_iota(jnp.int32, lf.shape, 1)
    lf_ref[...] = jnp.where((pos >= first_valid) & (lane < H_B), lf, 0.0)


def _proj_even(x, w6, wf, bf, *, nb, seq, tm, first_valid):
    nt = seq // tm
    rows = nb * seq
    row_spec = lambda c: pl.BlockSpec((tm, c), lambda b, j: (b * nt + j, 0))
    full = lambda a: pl.BlockSpec(a.shape, lambda b, j: (0,) * a.ndim)
    outs = [jax.ShapeDtypeStruct((rows, 512), F32)] * 6 + [jax.ShapeDtypeStruct((rows, LANES), F32)]
    return pl.pallas_call(
        functools.partial(_proj_even_kernel, tm=tm, first_valid=first_valid),
        grid=(nb, nt),
        in_specs=[row_spec(D_MODEL), full(w6), full(wf), full(bf)],
        out_specs=[row_spec(512)] * 6 + [row_spec(LANES)],
        out_shape=outs,
        compiler_params=_params(("parallel", "parallel")),
        name="proj_even",
    )(x, w6, wf, bf)


def _proj_odd_kernel(x_ref, w_ref, gam_ref, qc_ref, kc_ref, lf_ref, vc_ref, gc_ref, qd_ref, kd_ref,
                     vd_ref, gd_ref, *, tm, first_valid):
    xb = x_ref[...].astype(BF16)
    g0 = gam_ref[0:1, :]
    g1 = gam_ref[1:2, :]
    mx = jnp.maximum(g0, g1)
    e0 = jnp.exp(g0 - mx)
    e1 = jnp.exp(g1 - mx)
    p0 = e0 / (e0 + e1)
    p1 = e1 / (e0 + e1)
    lb = (p0 + p1) - p0
    pos = pl.program_id(1) * tm + lax.broadcasted_iota(jnp.int32, (tm, 1), 0)
    valid = pos >= first_valid
    qc_ref[...] = _dot(xb, w_ref[:, 0:512])
    fc = _dot(xb, w_ref[:, 512:1024])
    f = lb + (1.0 - lb) * (1.0 / (1.0 + jnp.exp(-fc)))
    kc_ref[...] = jnp.where(valid, 1.0 - f, 0.0)
    lf_ref[...] = jnp.where(valid, jnp.log(f), 0.0)
    vc_ref[...] = _dot(xb, w_ref[:, 1024:1536])
    gc_ref[...] = _dot(xb, w_ref[:, 1536:2048])
    qd_ref[...] = _dot(xb, w_ref[:, 2048:2560])
    kd_ref[...] = _dot(xb, w_ref[:, 2560:2688])
    vd_ref[...] = _dot(xb, w_ref[:, 2688:2816])
    gd_ref[...] = _dot(xb, w_ref[:, 2816:3328])


def _proj_odd(x, w, gam, *, nb, seq, tm, first_valid):
    nt = seq // tm
    rows = nb * seq
    row_spec = lambda c: pl.BlockSpec((tm, c), lambda b, j: (b * nt + j, 0))
    full = lambda a: pl.BlockSpec(a.shape, lambda b, j: (0,) * a.ndim)
    widths = (512, 512, 512, 512, 512, 512, 128, 128, 512)
    return pl.pallas_call(
        functools.partial(_proj_odd_kernel, tm=tm, first_valid=first_valid),
        grid=(nb, nt),
        in_specs=[row_spec(D_MODEL), full(w), full(gam)],
        out_specs=[row_spec(c) for c in widths],
        out_shape=[jax.ShapeDtypeStruct((rows, c), F32) for c in widths],
        compiler_params=_params(("parallel", "parallel")),
        name="proj_odd",
    )(x, w, gam)


def _out_ln_kernel(a_ref, b_ref, x_ref, wa_ref, wb_ref, g_ref, bb_ref, o_ref, *, tm, first_valid):
    acc = _dot(a_ref[...].astype(BF16), wa_ref[...]) + _dot(b_ref[...].astype(BF16), wb_ref[...])
    y = ALPHA * x_ref[...] + acc
    mu = jnp.mean(y, axis=-1, keepdims=True)
    yc = y - mu
    var = jnp.mean(yc * yc, axis=-1, keepdims=True)
    o = yc * lax.rsqrt(var + LN_EPS) * g_ref[...] + bb_ref[...]
    pos = pl.program_id(1) * tm + lax.broadcasted_iota(jnp.int32, (tm, 1), 0)
    o_ref[...] = jnp.where(pos >= first_valid, o, 0.0)


def _out_ln(a, b, x, wa, wb, g, bb, *, nb, seq, tm, first_valid):
    nt = seq // tm
    row_spec = lambda c: pl.BlockSpec((tm, c), lambda i, j: (i * nt + j, 0))
    full = lambda t: pl.BlockSpec(t.shape, lambda i, j: (0,) * t.ndim)
    return pl.pallas_call(
        functools.partial(_out_ln_kernel, tm=tm, first_valid=first_valid),
        grid=(nb, nt),
        in_specs=[row_spec(512), row_spec(512), row_spec(D_MODEL), full(wa), full(wb), full(g), full(bb)],
        out_specs=row_spec(D_MODEL),
        out_shape=jax.ShapeDtypeStruct(x.shape, F32),
        compiler_params=_params(("parallel", "parallel")),
        name="out_ln",
    )(a, b, x, wa, wb, g, bb)


def _pool_kernel(u_ref, g_ref, wp_ref, sc_ref, y_ref, buf_ref, *, tr, nsub, first_valid):
    j = pl.program_id(1)
    pos = lax.broadcasted_iota(jnp.int32, (tr, 1), 0)
    if nsub == 1:
        pos = pos + j * tr
    for q in range(nsub):
        rs = slice(q * tr, (q + 1) * tr)
        if nsub == 1:
            @pl.when(j == 0)
            def _():
                buf_ref[0:W_MAX, :] = jnp.zeros((W_MAX, D_A), F32)

            @pl.when(j > 0)
            def _():
                buf_ref[0:W_MAX, :] = buf_ref[tr:tr + W_MAX, :]
        else:
            buf_ref[0:W_MAX, :] = jnp.zeros((W_MAX, D_A), F32)
        uf = jnp.where(pos >= first_valid, u_ref[rs, :], 0.0)
        buf_ref[W_MAX:W_MAX + tr, :] = uf
        for g, w in enumerate(POOL_WINDOWS):
            cs = slice(g * GC, (g + 1) * GC)
            s = buf_ref[W_MAX:W_MAX + tr, cs]
            for i in range(1, w):
                s = s + buf_ref[W_MAX - i:W_MAX - i + tr, cs]
            cnt = jnp.clip(pos - first_valid + 1, 1, w).astype(F32)
            d = s / cnt - uf[:, cs]
            y = _dot(d, wp_ref[g]) * sc_ref[:, cs]
            y_ref[rs, cs] = (y * _silu(g_ref[rs, cs])).astype(y_ref.dtype)


def _pool(u, g, wp, sc, *, nb, seq, tr, first_valid, nsub=1, ydt=F32):
    nt = seq // tr
    assert nsub == 1 or (nt == 1 and nb % nsub == 0)
    nb = nb // nsub
    row_spec = pl.BlockSpec((nsub * tr, D_A), lambda b, j: (b * nt + j, 0))
    return pl.pallas_call(
        functools.partial(_pool_kernel, tr=tr, nsub=nsub, first_valid=first_valid),
        grid=(nb, nt),
        in_specs=[row_spec, row_spec,
                  pl.BlockSpec(wp.shape, lambda b, j: (0, 0, 0)),
                  pl.BlockSpec(sc.shape, lambda b, j: (0, 0))],
        out_specs=row_spec,
        out_shape=jax.ShapeDtypeStruct(u.shape, ydt),
        scratch_shapes=[pltpu.VMEM((tr + W_MAX, D_A), F32)],
        compiler_params=_params(("parallel", "arbitrary")),
        name="pool",
    )(u, g, wp, sc)


def _fcum_kernel(lf_ref, aq_ref, ak_ref, *, nt):
    t = lax.broadcasted_iota(jnp.int32, (BLOCK, BLOCK), 0)
    lane = lax.broadcasted_iota(jnp.int32, (BLOCK, BLOCK), 1)
    tri = jnp.where(t >= lane, 1.0, 0.0).astype(F32)
    carry = jnp.zeros((1, LANES), F32)
    for j in range(nt):
        rs = slice(j * BLOCK, (j + 1) * BLOCK)
        f = _dot3(tri, lf_ref[rs, :]) + carry
        carry = f[BLOCK - 1:BLOCK, :]
        fh, fm, fl = _split3(f)
        f3 = fh + pltpu.roll(fm, H_B, 1) + pltpu.roll(fl, 2 * H_B, 1)
        aq_ref[rs, :] = jnp.where((lane >= 3 * H_B) & (lane < 6 * H_B), 1.0, f3)
        ak_ref[rs, :] = jnp.where(lane < 3 * H_B, 1.0, -pltpu.roll(f3, 3 * H_B, 1))


def _fcum(lf, *, nb, seq):
    spec = pl.BlockSpec((seq, LANES), lambda b: (b, 0))
    sds = jax.ShapeDtypeStruct(lf.shape, F32)
    return pl.pallas_call(
        functools.partial(_fcum_kernel, nt=seq // BLOCK),
        grid=(nb,),
        in_specs=[spec],
        out_specs=[spec, spec],
        out_shape=[sds, sds],
        compiler_params=_params(("parallel",)),
        name="fcum",
    )(lf)


def _fox_prompt_kernel(q_ref, k_ref, v_ref, aq_ref, ak_ref, gb_ref, y_ref, *, first_valid, klens):
    i = pl.program_id(1)
    lane = lax.broadcasted_iota(jnp.int32, (BLOCK, LANES), 1)
    low = lane < DH_B
    row = lax.broadcasted_iota(jnp.int32, (2 * BLOCK, 1), 0)
    qpos = i * BLOCK + (row & (BLOCK - 1))

    def attend(lk):
        kpos = lax.broadcasted_iota(jnp.int32, (1, lk), 1)
        mask = (kpos <= qpos) & (kpos >= first_valid)
        aq = aq_ref[...]
        ak = ak_ref[0:lk, :]
        for j in range(H_B // 2):
            cs = slice(j * LANES, (j + 1) * LANES)
            qp = q_ref[:, cs] * (DH_B ** -0.5)
            a0 = jnp.where(((lane & (H_B - 1)) == 2 * j) & (lane < 6 * H_B), aq, 0.0)
            a1 = jnp.where(((lane & (H_B - 1)) == 2 * j + 1) & (lane < 6 * H_B), aq, 0.0)
            qs = jnp.concatenate([
                jnp.concatenate([jnp.where(low, qp, 0.0), a0], axis=1),
                jnp.concatenate([jnp.where(low, 0.0, qp), a1], axis=1)], axis=0)
            ks = jnp.concatenate([k_ref[0:lk, cs], ak], axis=1)
            s = jnp.where(mask, _dot_nt(qs, ks), NEG)
            m = jnp.max(s, axis=-1, keepdims=True)
            p = jnp.exp(s - m)
            l = jnp.sum(p, axis=-1, keepdims=True)
            r = _dot(p, v_ref[0:lk, cs]) / l
            o = jnp.where(low, r[:BLOCK], r[BLOCK:])
            y_ref[:, cs] = (o * _silu(gb_ref[:, cs])).astype(y_ref.dtype)

    lo = 0
    for hi in klens:
        @pl.when((i >= lo) & (i < hi))
        def _(hi=hi):
            attend(hi * BLOCK)
        lo = hi


def _fox_prompt(q, k, v, aq, ak, gb, *, nb, seq, first_valid, ydt=F32):
    nt = seq // BLOCK
    nvar = min(FOX_KEY_VARIANTS, nt)
    klens = tuple(-(-nt * (x + 1) // nvar) for x in range(nvar))
    blk = lambda c: pl.BlockSpec((BLOCK, c), lambda b, i: (b * nt + i, 0))
    whole = lambda c: pl.BlockSpec((seq, c), lambda b, i: (b, 0))
    return pl.pallas_call(
        functools.partial(_fox_prompt_kernel, first_valid=first_valid, klens=klens),
        grid=(nb, nt),
        in_specs=[blk(D_B), whole(D_B), whole(D_B), blk(LANES), whole(LANES), blk(D_B)],
        out_specs=blk(D_B),
        out_shape=jax.ShapeDtypeStruct(q.shape, ydt),
        compiler_params=_params(("parallel", "parallel")),
        name="fox_prompt",
    )(q, k, v, aq, ak, gb)


def _fox_page_copies(pt_ref, kt_hbm, vt_hbm, lf_hbm, kbuf, vbuf, lbuf, sem, seq, step, slot, *, gp, npages):
    copies = []
    for g in range(gp):
        page = pt_ref[seq, npages - (step + 1) * gp + g]
        copies.append(pltpu.make_async_copy(kt_hbm.at[0, page], kbuf.at[slot, g], sem.at[slot]))
        copies.append(pltpu.make_async_copy(vt_hbm.at[0, page], vbuf.at[slot, g], sem.at[slot]))
        copies.append(pltpu.make_async_copy(lf_hbm.at[0, page], lbuf.at[slot, g], sem.at[slot]))
    return copies


def _fox_sample_kernel(pt_ref, *refs, gp, npages, side):
    q_ref, kn_ref, vn_ref, lft_ref, gb_ref, ut_ref, kt_hbm, vt_hbm, lf_hbm = refs[:9]
    n_in = 9 + (14 if side else 0)
    n_out = 1 + (3 if side else 0)
    y_ref = refs[n_in]
    qbd_ref, m_ref, l_ref, acc_ref, suf_ref, kbuf, vbuf, lbuf, sem = refs[n_in + n_out:n_in + n_out + 9]
    seq = pl.program_id(0)
    step = pl.program_id(1)
    nseq = pl.num_programs(0)
    steps = npages // gp
    lin = seq * steps + step
    copies = functools.partial(_fox_page_copies, pt_ref, kt_hbm, vt_hbm, lf_hbm, kbuf, vbuf, lbuf, sem,
                               gp=gp, npages=npages)

    def start(cs):
        for idx, c in enumerate(cs):
            c.start(priority=idx % 2)

    @pl.when(lin == 0)
    def _():
        for d in range(FOX_SLOTS - 1):
            start(copies(0, d, d))

    ahead = step + (FOX_SLOTS - 1)
    wrap = (ahead >= steps).astype(jnp.int32)
    seq_a = seq + wrap
    step_a = ahead - wrap * steps

    @pl.when(seq_a < nseq)
    def _():
        start(copies(seq_a, step_a, lax.rem(lin + (FOX_SLOTS - 1), FOX_SLOTS)))

    if side:
        hg_in, sw_in = refs[9:16], refs[16:23]
        y_hg, s_hg, y_sw = refs[n_in + 1:n_in + 4]
        st_hg = refs[n_in + n_out + 9]
        hu = lax.div(lin, side["hg_every"])

        @pl.when((lax.rem(lin, side["hg_every"]) == 0) & (hu < side["hg_units"]))
        def _():
            j = lax.rem(hu, side["nc"])
            _hgrn_step(*hg_in, None, y_hg, s_hg, st_hg, j == 0, j == side["nc"] - 1, c=HGRN_CHUNK, nsub=1)

        su = lax.div(lin, side["sw_every"])

        @pl.when((lax.rem(lin, side["sw_every"]) == 0) & (su < side["sw_units"]))
        def _():
            _swa_prompt_step(*sw_in, y_sw, lax.rem(su, side["nt"]), first_valid=side["first_valid"])

    slot = lax.rem(lin, FOX_SLOTS)
    for c in copies(seq, step, slot):
        c.wait()

    lane512 = lax.broadcasted_iota(jnp.int32, (SUBLANES, D_B), 1)
    nrow = H_B * SUBLANES

    @pl.when(step == 0)
    def _():
        qs = q_ref[...] * (DH_B ** -0.5)
        for h in range(H_B):
            qbd_ref[h * 8:(h + 1) * 8, :] = jnp.where((lane512 >> 6) == h, qs, 0.0)
        lft = lft_ref[0]
        lane = lax.broadcasted_iota(jnp.int32, (H_B, LANES), 1)
        gcum = jnp.zeros((H_B, LANES), F32)
        for u in range(SUBLANES):
            gcum = gcum + jnp.where(lane >= u, jnp.broadcast_to(lft[:, u:u + 1], (H_B, LANES)), 0.0)
        kpad = jnp.concatenate([kn_ref[...], jnp.zeros((LANES - 8, D_B), F32)], axis=0)
        vpad = jnp.concatenate([vn_ref[...], jnp.zeros((LANES - 8, D_B), F32)], axis=0)
        s = _dot_nt(qbd_ref[...], kpad)
        bias = jnp.concatenate(
            [jnp.broadcast_to(gcum[h:h + 1, :], (8, LANES)) for h in range(H_B)], axis=0)
        t = lax.broadcasted_iota(jnp.int32, (nrow, LANES), 0) & 7
        sl = lax.broadcasted_iota(jnp.int32, (nrow, LANES), 1)
        s = jnp.where(sl <= t, s - bias, NEG)
        m = jnp.max(s, axis=-1, keepdims=True)
        p = jnp.exp(s - m)
        m_ref[...] = m
        l_ref[...] = jnp.sum(p, axis=-1, keepdims=True)
        acc_ref[...] = _dot(p, vpad)
        suf_ref[...] = jnp.zeros(suf_ref.shape, F32)

    lfs = jnp.concatenate([lbuf[slot, g] for g in range(gp)], axis=0)
    wt = _dot3_rhs(lfs, ut_ref[...])
    suf = suf_ref[...]
    biases = [None] * gp
    for g in range(gp - 1, -1, -1):
        biases[g] = wt[g * 8:(g + 1) * 8, :LANES] + suf
        suf = suf + wt[g * 8:(g + 1) * 8, LANES:]
    suf_ref[...] = suf

    qbd = qbd_ref[...]
    ss = []
    for g in range(gp):
        sg = _dot(qbd, kbuf[slot, g])
        bg = jnp.concatenate(
            [jnp.broadcast_to(biases[g][h:h + 1, :], (8, LANES)) for h in range(H_B)], axis=0)
        ss.append(sg + bg)
    m_prev = m_ref[...]
    m_cur = ss[0]
    for g in range(1, gp):
        m_cur = jnp.maximum(m_cur, ss[g])
    m_new = jnp.maximum(m_prev, jnp.max(m_cur, axis=-1, keepdims=True))
    alpha = jnp.exp(m_prev - m_new)
    psum = None
    pv = None
    for g in range(gp):
        p = jnp.exp(ss[g] - m_new)
        psum = p if psum is None else psum + p
        d = _dot_nt(p, vbuf[slot, g])
        pv = d if pv is None else pv + d
    m_ref[...] = m_new
    l_ref[...] = alpha * l_ref[...] + jnp.sum(psum, axis=-1, keepdims=True)
    acc_ref[...] = alpha * acc_ref[...] + pv

    @pl.when(step == steps - 1)
    def _():
        o = acc_ref[...] / l_ref[...]
        out = jnp.zeros((SUBLANES, D_B), F32)
        for h in range(H_B):
            out = out + jnp.where((lane512 >> 6) == h, o[h * 8:(h + 1) * 8, :], 0.0)
        y_ref[...] = out * _silu(gb_ref[...])


def _fox_sample(page_table, q, kn, vn, lft, gb, kt, vt, lfc, *, gp, hgrn_args=None, swa_args=None,
                nb=0, seq=0, first_valid=0, ydt=F32):
    n, npages = page_table.shape
    steps = npages // gp
    total = n * steps
    assert npages % gp == 0 and steps >= FOX_SLOTS - 1
    s_i = np.arange(LANES)
    ut = np.concatenate([(s_i[:, None] > s_i[None, :]).astype(np.float32),
                         np.ones((LANES, LANES), np.float32)], axis=1)
    row = lambda c: pl.BlockSpec((8, c), lambda i, s, pt: (i, 0))
    hbm = pl.BlockSpec(memory_space=pl.ANY)
    const = lambda a: pl.BlockSpec(a.shape, lambda i, s, pt: (0,) * a.ndim)
    in_specs = [row(D_B), row(D_B), row(D_B),
                pl.BlockSpec((1, H_B, LANES), lambda i, s, pt: (i, 0, 0)),
                row(D_B), const(ut), hbm, hbm, hbm]
    args = [q, kn, vn, lft, gb, jnp.asarray(ut), kt, vt, lfc]
    out_specs = [pl.BlockSpec((8, D_B), lambda i, s, pt: (i, 0))]
    out_shape = [jax.ShapeDtypeStruct(q.shape, F32)]
    nrow = H_B * SUBLANES
    scratch = [pltpu.VMEM((nrow, D_B), F32), pltpu.VMEM((nrow, 1), F32),
               pltpu.VMEM((nrow, 1), F32), pltpu.VMEM((nrow, D_B), F32),
               pltpu.VMEM((H_B, LANES), F32),
               pltpu.VMEM((FOX_SLOTS, gp, D_B, LANES), F32),
               pltpu.VMEM((FOX_SLOTS, gp, D_B, LANES), F32),
               pltpu.VMEM((FOX_SLOTS, gp, H_B, LANES), F32),
               pltpu.SemaphoreType.DMA((FOX_SLOTS,))]
    side = None
    if hgrn_args is not None:
        nc, nt = seq // HGRN_CHUNK, seq // BLOCK
        hg_units, sw_units = nb * nc, nb * nt
        assert total >= hg_units and total >= sw_units
        side = dict(hg_every=total // hg_units, hg_units=hg_units, nc=nc,
                    sw_every=total // sw_units, sw_units=sw_units, nt=nt, first_valid=first_valid)

        def unit(every, units):
            return lambda i, s, pt: jnp.minimum(lax.div(i * steps + s, every), units - 1)

        hu, su = unit(side["hg_every"], hg_units), unit(side["sw_every"], sw_units)
        hq, hk, hlf, hv, hg, hng = hgrn_args
        ssk, sq, sk_, sv, sg = swa_args
        ms = jnp.asarray(_hgrn_mstack(HGRN_CHUNK))
        hblk = pl.BlockSpec((HGRN_CHUNK, HK_C), lambda i, s, pt: (hu(i, s, pt), 0))
        st_spec = pl.BlockSpec((1, H_C, DK_C, DV_C), lambda i, s, pt: (lax.div(hu(i, s, pt), nc), 0, 0, 0))
        cur = lambda c: pl.BlockSpec((BLOCK, c), lambda i, s, pt: (su(i, s, pt), 0))

        def prev_idx(i, s, pt):
            u = su(i, s, pt)
            return u - jnp.minimum(lax.rem(u, nt), 1)

        prev = lambda c: pl.BlockSpec((BLOCK, c), lambda i, s, pt: (prev_idx(i, s, pt), 0))
        kvw = KV_D * DH_D
        in_specs += [hblk] * 5 + [const(ms), const(hng)]
        in_specs += [pl.BlockSpec(memory_space=pltpu.SMEM), cur(D_D), prev(kvw), cur(kvw), prev(kvw),
                     cur(kvw), cur(D_D)]
        args += [hq, hk, hlf, hv, hg, ms, hng, ssk, sq, sk_, sk_, sv, sv, sg]
        out_specs += [hblk, st_spec, cur(D_D)]
        out_shape += [jax.ShapeDtypeStruct(hq.shape, ydt),
                      jax.ShapeDtypeStruct((nb, H_C, DK_C, DV_C), F32),
                      jax.ShapeDtypeStruct(sq.shape, ydt)]
        scratch.append(pltpu.VMEM((H_C, DV_C, DK_C), F32))
    grid_spec = pltpu.PrefetchScalarGridSpec(
        num_scalar_prefetch=1, grid=(n, steps), in_specs=in_specs,
        out_specs=out_specs if side else out_specs[0], scratch_shapes=scratch)
    return pl.pallas_call(
        functools.partial(_fox_sample_kernel, gp=gp, npages=npages, side=side),
        grid_spec=grid_spec,
        out_shape=out_shape if side else out_shape[0],
        compiler_params=_params(("arbitrary", "arbitrary")),
        name="fox_sample",
    )(page_table, *args)


def _hgrn_levels(c):
    out = []
    m = c // 2
    while m >= 1:
        out.append(m)
        m //= 2
    return tuple(out)


def _hgrn_mstack(c):
    t = np.arange(c)[:, None]
    u = np.arange(c)[None, :]
    mats = [(u <= t)]
    for m in _hgrn_levels(c):
        r = (t // (2 * m)) * (2 * m) + m - 1
        second = (t % (2 * m)) >= m
        mats.append(np.where(second, (u > r) & (u <= t), (u > t) & (u <= r)))
    return np.concatenate(mats, axis=0).astype(np.float32)


def _hgrn_kernel(*refs, c, nsub, has_s0):
    if has_s0:
        q_ref, k_ref, lf_ref, v_ref, g_ref, m_ref, ng_ref, s0_ref, y_ref, so_ref, st_ref = refs
    else:
        q_ref, k_ref, lf_ref, v_ref, g_ref, m_ref, ng_ref, y_ref, so_ref, st_ref = refs
        s0_ref = None
    ci = pl.program_id(1)
    _hgrn_step(q_ref, k_ref, lf_ref, v_ref, g_ref, m_ref, ng_ref, s0_ref, y_ref, so_ref, st_ref,
               ci == 0, ci == pl.num_programs(1) - 1, c=c, nsub=nsub)


def _hgrn_step(q_ref, k_ref, lf_ref, v_ref, g_ref, m_ref, ng_ref, s0_ref, y_ref, so_ref, st_ref,
               first, last, *, c, nsub):
    has_s0 = s0_ref is not None
    levels = _hgrn_levels(c)
    units = [(q, h) for q in range(nsub) for h in range(H_C)]

    @pl.when(first)
    def _():
        for q, h in units:
            st_ref[q * H_C + h] = s0_ref[q, h].T if has_s0 else jnp.zeros((DV_C, DK_C), F32)

    t_col = lax.broadcasted_iota(jnp.int32, (c, 1), 0)
    t_row = lax.broadcasted_iota(jnp.int32, (c, c), 0)
    s_col = lax.broadcasted_iota(jnp.int32, (c, c), 1)
    rpad = max(LANES - c, 0)
    mstack = m_ref[...]
    sub = {}
    for q in range(nsub):
        rs = slice(q * c, (q + 1) * c)
        e = _dot3(mstack, lf_ref[rs, :])
        b = e[0:c]
        qv, kv = q_ref[rs, :], k_ref[rs, :]
        bl = b[c - 1:c, :]
        qx, kx = [], []
        for li, m in enumerate(levels):
            x = jnp.exp(e[(li + 1) * c:(li + 2) * c])
            second = (t_col & (2 * m - 1)) >= m
            qx.append(jnp.where(second, qv * x, 0.0))
            kx.append(jnp.where(second, 0.0, kv * x))
        sub[q] = dict(rs=rs, qb=qv * jnp.exp(b), qk=qv * kv, kp=kv * jnp.exp(bl - b), dec=jnp.exp(bl),
                      qx=qx, kx=kx, v=v_ref[rs, :])
    o_inter, a_mat = {}, {}
    for q, h in units:
        cs = slice(h * DK_C, (h + 1) * DK_C)
        d = sub[q]
        o_inter[q, h] = _dot_nt(d["qb"][:, cs], st_ref[q * H_C + h])
        a = jnp.where(t_row == s_col, jnp.sum(d["qk"][:, cs], axis=-1, keepdims=True), 0.0)
        for li, m in enumerate(levels):
            sh = (2 * m).bit_length() - 1
            am = _dot_nt(d["qx"][li][:, cs], d["kx"][li][:, cs])
            a = a + jnp.where((t_row >> sh) == (s_col >> sh), am, 0.0)
        a_mat[q, h] = a
    for q, h in units:
        cs = slice(h * DK_C, (h + 1) * DK_C)
        d = sub[q]
        v = d["v"][:, cs]
        kp = d["kp"][:, cs]
        o = o_inter[q, h] + _dot(a_mat[q, h], v)
        if rpad:
            v = jnp.concatenate([v, jnp.zeros((rpad, DV_C), F32)], axis=0)
            kp = jnp.concatenate([kp, jnp.zeros((rpad, DK_C), F32)], axis=0)
        st_ref[q * H_C + h] = st_ref[q * H_C + h] * d["dec"][:, cs] + _dot(v.T, kp)
        y = o * lax.rsqrt(jnp.mean(o * o, axis=-1, keepdims=True) + LN_EPS) * ng_ref[...]
        y_ref[d["rs"], cs] = (y * _silu(g_ref[d["rs"], cs])).astype(y_ref.dtype)

    @pl.when(last)
    def _():
        for q, h in units:
            so_ref[q, h] = st_ref[q * H_C + h].T


def _hgrn(q, k, lf, v, g, ng, s0, *, nb, seq, c, nsub=1, ydt=F32):
    nc = seq // c
    assert nsub == 1 or (nc == 1 and nb % nsub == 0)
    ms = jnp.asarray(_hgrn_mstack(c))
    blk = pl.BlockSpec((nsub * c, HK_C), lambda b, j: (b * nc + j, 0))
    st_spec = pl.BlockSpec((nsub, H_C, DK_C, DV_C), lambda b, j: (b, 0, 0, 0))
    in_specs = [blk, blk, blk, blk, blk,
                pl.BlockSpec(ms.shape, lambda b, j: (0, 0)),
                pl.BlockSpec(ng.shape, lambda b, j: (0, 0))]
    args = [q, k, lf, v, g, ms, ng]
    if s0 is not None:
        in_specs.append(st_spec)
        args.append(s0)
    return pl.pallas_call(
        functools.partial(_hgrn_kernel, c=c, nsub=nsub, has_s0=s0 is not None),
        grid=(nb // nsub, nc),
        in_specs=in_specs,
        out_specs=[blk, st_spec],
        out_shape=[jax.ShapeDtypeStruct(q.shape, ydt),
                   jax.ShapeDtypeStruct((nb, H_C, DK_C, DV_C), F32)],
        scratch_shapes=[pltpu.VMEM((nsub * H_C, DV_C, DK_C), F32)],
        compiler_params=_params(("parallel", "arbitrary")),
        name="hgrn",
    )(*args)


def _swa_prompt_kernel(sk_ref, q_ref, kp_ref, kc_ref, vp_ref, vc_ref, g_ref, y_ref, *, first_valid):
    _swa_prompt_step(sk_ref, q_ref, kp_ref, kc_ref, vp_ref, vc_ref, g_ref, y_ref, pl.program_id(1),
                     first_valid=first_valid)


def _swa_prompt_step(sk_ref, q_ref, kp_ref, kc_ref, vp_ref, vc_ref, g_ref, y_ref, i, *, first_valid):
    lane = lax.broadcasted_iota(jnp.int32, (BLOCK, LANES), 1)
    low = lane < DH_D
    kk = jnp.concatenate([kp_ref[...], kc_ref[...]], axis=0)
    vv = jnp.concatenate([vp_ref[...], vc_ref[...]], axis=0)
    kk_sw = pltpu.roll(kk, DH_D, 1)
    vv_sw = pltpu.roll(vv, DH_D, 1)
    kj = lax.broadcasted_iota(jnp.int32, (BLOCK, 2 * BLOCK), 1)
    tq = lax.broadcasted_iota(jnp.int32, (BLOCK, 2 * BLOCK), 0)
    dist = tq + BLOCK - kj
    kpos = i * BLOCK - BLOCK + kj
    mask = (dist >= 0) & (dist <= WINDOW) & (kpos >= first_valid)
    distf = dist.astype(F32)
    ss = []
    for h in range(H_D):
        j, e, kvh = h // 2, h % 2, h // G_D
        qp = q_ref[:, j * LANES:(j + 1) * LANES] * (DH_D ** -0.5)
        qm = jnp.where(low, qp, 0.0) if e == 0 else jnp.where(low, 0.0, qp)
        s = _dot_nt(qm, kk if e == kvh else kk_sw) - (2.0 ** -(h + 1)) * distf
        ss.append(jnp.where(mask, s, NEG))
    ps, dens = [], []
    for h in range(H_D):
        sink = sk_ref[h]
        m = jnp.maximum(jnp.max(ss[h], axis=-1, keepdims=True), sink)
        p = jnp.exp(ss[h] - m)
        ps.append(p)
        dens.append(jnp.sum(p, axis=-1, keepdims=True) + jnp.exp(sink - m))
    outs = [_dot(ps[h], vv if h % 2 == h // G_D else vv_sw) / dens[h] for h in range(H_D)]
    for j in range(H_D // 2):
        cs = slice(j * LANES, (j + 1) * LANES)
        y_ref[:, cs] = (jnp.where(low, outs[2 * j], outs[2 * j + 1]) * _silu(g_ref[:, cs])).astype(y_ref.dtype)


def _swa_prompt(sinks, q, k, v, g, *, nb, seq, first_valid, ydt=F32):
    nt = seq // BLOCK
    cur = lambda c: pl.BlockSpec((BLOCK, c), lambda b, i: (b * nt + i, 0))
    prev = lambda c: pl.BlockSpec((BLOCK, c), lambda b, i: (b * nt + jnp.maximum(i - 1, 0), 0))
    kvw = KV_D * DH_D
    return pl.pallas_call(
        functools.partial(_swa_prompt_kernel, first_valid=first_valid),
        grid=(nb, nt),
        in_specs=[pl.BlockSpec(memory_space=pltpu.SMEM),
                  cur(D_D), prev(kvw), cur(kvw), prev(kvw), cur(kvw), cur(D_D)],
        out_specs=cur(D_D),
        out_shape=jax.ShapeDtypeStruct(q.shape, ydt),
        compiler_params=_params(("parallel", "parallel")),
        name="swa_prompt",
    )(sinks, q, k, k, v, v, g)


def _swa_sample_kernel(sk_ref, q_ref, kn_ref, vn_ref, kt_ref, vt_ref, g_ref, y_ref, *, nsub):
    nrow = H_D * SUBLANES
    lane8 = lax.broadcasted_iota(jnp.int32, (SUBLANES, LANES), 1)
    rows = lax.broadcasted_iota(jnp.int32, (nrow, 1), 0)
    hrow = rows >> 3
    t = rows & 7
    slope = jnp.zeros((nrow, 1), F32)
    sink = jnp.zeros((nrow, 1), F32)
    for h in range(H_D):
        slope = jnp.where(hrow == h, 2.0 ** -(h + 1), slope)
        sink = jnp.where(hrow == h, sk_ref[h], sink)
    kj = lax.broadcasted_iota(jnp.int32, (nrow, LANES), 1)
    d_old = t + WINDOW - kj
    d_new = t - kj
    for u in range(nsub):
        rs = slice(u * 8, (u + 1) * 8)
        blocks = []
        for h in range(H_D):
            j, e, kvh = h // 2, h % 2, h // G_D
            qp = q_ref[rs, j * LANES:(j + 1) * LANES] * (DH_D ** -0.5)
            src = qp if e == kvh else pltpu.roll(qp, DH_D, 1)
            blocks.append(jnp.where((lane8 >> 6) == kvh, src, 0.0))
        qbd = jnp.concatenate(blocks, axis=0)
        kpad = jnp.concatenate([kn_ref[rs, :], jnp.zeros((LANES - 8, LANES), F32)], axis=0)
        vpad = jnp.concatenate([vn_ref[rs, :], jnp.zeros((LANES - 8, LANES), F32)], axis=0)
        s_old = _dot(qbd, kt_ref[0, u]) - slope * d_old.astype(F32)
        s_old = jnp.where(d_old <= WINDOW, s_old, NEG)
        s_new = _dot_nt(qbd, kpad) - slope * d_new.astype(F32)
        s_new = jnp.where(d_new >= 0, s_new, NEG)
        m = jnp.maximum(jnp.maximum(jnp.max(s_old, axis=-1, keepdims=True),
                                    jnp.max(s_new, axis=-1, keepdims=True)), sink)
        p_old = jnp.exp(s_old - m)
        p_new = jnp.exp(s_new - m)
        den = (jnp.sum(p_old, axis=-1, keepdims=True) + jnp.sum(p_new, axis=-1, keepdims=True)
               + jnp.exp(sink - m))
        o = (_dot_nt(p_old, vt_ref[0, u]) + _dot(p_new, vpad)) / den
        o_sw = pltpu.roll(o, DH_D, 1)
        for j in range(H_D // 2):
            parts = []
            for e in range(2):
                h = 2 * j + e
                parts.append((o if e == h // G_D else o_sw)[h * 8:(h + 1) * 8, :])
            cs = slice(j * LANES, (j + 1) * LANES)
            y_ref[rs, cs] = jnp.where(lane8 < DH_D, parts[0], parts[1]) * _silu(g_ref[rs, cs])


def _swa_sample(sinks, q, kn, vn, kt, vt, g, *, nsub):
    n = q.shape[0] // 8
    assert n % nsub == 0
    row = lambda c: pl.BlockSpec((nsub * 8, c), lambda i: (i, 0))
    buf = pl.BlockSpec((1, nsub, LANES, LANES), lambda i: (0, i, 0, 0))
    kvw = KV_D * DH_D
    return pl.pallas_call(
        functools.partial(_swa_sample_kernel, nsub=nsub),
        grid=(n // nsub,),
        in_specs=[pl.BlockSpec(memory_space=pltpu.SMEM), row(D_D), row(kvw), row(kvw), buf, buf, row(D_D)],
        out_specs=row(D_D),
        out_shape=jax.ShapeDtypeStruct(q.shape, F32),
        compiler_params=_params(("parallel",)),
        name="swa_sample",
    )(sinks, q, kn, vn, kt, vt, g)


def kernel(x_prompt, x_sample, cache_pool, cache_k, cache_v, cache_logf, state_hgrn, cache_win_k,
           cache_win_v, page_table, meta_tokens, w_in_e, b_f, w_pool, pool_scale, w_out_e, ln_g_e,
           ln_b_e, w_in_o, hgrn_gamma, hgrn_norm_g, sinks, w_out_o, ln_g_o, ln_b_o):
    assert w_in_e.shape[0] == 1 and w_in_o.shape[0] == 1 and hgrn_gamma.shape[0] == DEPTH
    nb, sp, _ = x_prompt.shape
    ns, ts, _ = x_sample.shape
    assert ts == 8 and sp % BLOCK == 0
    lp = sp + BLOCK
    n_pool = cache_k.shape[1]
    wb = cache_win_k.shape[2]
    assert wb == WINDOW and cache_k.shape[2] == LANES
    ptm = lp // 4
    nsub = max(d for d in range(1, SEQS_PER_STEP + 1) if ns % d == 0)
    stm = min(256, ns * ts)
    assert lp % (4 * SUBLANES) == 0 and (ns * ts) % stm == 0

    hp = jnp.concatenate([jnp.zeros((nb, PAD, D_MODEL), F32),
                          jnp.broadcast_to(meta_tokens, (nb, N_META, D_MODEL)), x_prompt], axis=1)
    hp = hp.reshape(nb * lp, D_MODEL)
    hs = x_sample.reshape(ns * ts, D_MODEL)

    we = w_in_e[0]
    w6 = we[:, :6 * 512].astype(BF16)
    wf = jnp.pad(we[:, 6 * 512:], ((0, 0), (0, LANES - H_B))).astype(BF16)
    bf = jnp.pad(b_f[0][None, :], ((0, 0), (0, LANES - H_B)))
    wa = w_out_e[0, :D_A].astype(BF16)
    wbm = w_out_e[0, D_A:].astype(BF16)
    sc = pool_scale[0][None, :]
    lng, lnb = ln_g_e[0][None, :], ln_b_e[0][None, :]

    u_p, ga_p, q_p, k_p, v_p, gb_p, lf_p = _proj_even(hp, w6, wf, bf, nb=nb, seq=lp, tm=ptm, first_valid=PAD)
    ya_p = _pool(u_p, ga_p, w_pool[0], sc, nb=nb, seq=lp, tr=ptm, first_valid=PAD, ydt=BF16)
    aq, ak = _fcum(lf_p, nb=nb, seq=lp)
    yb_p = _fox_prompt(q_p, k_p, v_p, aq, ak, gb_p, nb=nb, seq=lp, first_valid=PAD, ydt=BF16)
    hp1 = _out_ln(ya_p, yb_p, hp, wa, wbm, lng, lnb, nb=nb, seq=lp, tm=ptm, first_valid=PAD)

    wo = w_in_o[0].astype(BF16)
    ng = hgrn_norm_g[0][None, :]
    sk = sinks[0]
    qc, kc, lfh, vc, gc, qd, kd, vd, gd = _proj_odd(hp1, wo, hgrn_gamma, nb=nb, seq=lp, tm=ptm, first_valid=PAD)

    u_s, ga_s, q_s, k_s, v_s, gb_s, lf_s = _proj_even(hs, w6, wf, bf, nb=1, seq=ns * ts, tm=stm, first_valid=0)
    hist = POOL_BUF + ts + 1
    u3 = u_s.reshape(ns, ts, D_A)
    u_ext = jnp.concatenate([cache_pool[0], u3, jnp.zeros((ns, 1, D_A), F32)], axis=1)
    g_ext = jnp.concatenate([jnp.zeros((ns, POOL_BUF, D_A), F32), ga_s.reshape(ns, ts, D_A),
                             jnp.zeros((ns, 1, D_A), F32)], axis=1)
    ya_s = _pool(u_ext.reshape(ns * hist, D_A), g_ext.reshape(ns * hist, D_A), w_pool[0], sc,
                 nb=ns, seq=hist, tr=hist, first_valid=0, nsub=nsub)
    ya_s = ya_s.reshape(ns, hist, D_A)[:, POOL_BUF:POOL_BUF + ts].reshape(ns * ts, D_A)
    kt = jnp.transpose(cache_k, (0, 1, 3, 4, 2)).reshape(1, n_pool, D_B, LANES)
    vt = jnp.transpose(cache_v, (0, 1, 3, 4, 2)).reshape(1, n_pool, D_B, LANES)
    lfc = jnp.transpose(cache_logf, (0, 1, 3, 2))
    lft = jnp.transpose(lf_s[:, :H_B].reshape(ns, ts, H_B), (0, 2, 1))
    lft = jnp.pad(lft, ((0, 0), (0, 0), (0, LANES - ts)))
    fox_steps = ns * (page_table.shape[1] // FOX_PAGES_PER_STEP)
    ride = fox_steps >= nb * (lp // HGRN_CHUNK)
    if ride:
        yb_s, yc_p, hg_p, yd_p = _fox_sample(
            page_table, q_s, k_s, v_s, lft, gb_s, kt, vt, lfc, gp=FOX_PAGES_PER_STEP,
            hgrn_args=(qc, kc, lfh, vc, gc, ng), swa_args=(sk, qd, kd, vd, gd),
            nb=nb, seq=lp, first_valid=PAD, ydt=BF16)
    else:
        yb_s = _fox_sample(page_table, q_s, k_s, v_s, lft, gb_s, kt, vt, lfc, gp=FOX_PAGES_PER_STEP)
        yc_p, hg_p = _hgrn(qc, kc, lfh, vc, gc, ng, None, nb=nb, seq=lp, c=HGRN_CHUNK, ydt=BF16)
        yd_p = _swa_prompt(sk, qd, kd, vd, gd, nb=nb, seq=lp, first_valid=PAD, ydt=BF16)
    hs1 = _out_ln(ya_s, yb_s, hs, wa, wbm, lng, lnb, nb=1, seq=ns * ts, tm=stm, first_valid=0)

    new_pool_p = u_p.reshape(nb, lp, D_A)[:, -POOL_BUF:][None]
    new_pool_s = u_ext[:, ts:ts + POOL_BUF][None]
    new_k_p = k_p.reshape(nb, lp, H_B, DH_B)[:, PAD:][None]
    new_v_p = v_p.reshape(nb, lp, H_B, DH_B)[:, PAD:][None]
    new_lf_p = lf_p.reshape(nb, lp, LANES)[:, PAD:, :H_B][None]
    new_k_s = k_s.reshape(ns, ts, H_B, DH_B)[None]
    new_v_s = v_s.reshape(ns, ts, H_B, DH_B)[None]
    new_lf_s = lf_s[:, :H_B].reshape(ns, ts, H_B)[None]

    wc = w_out_o[0, :D_C].astype(BF16)
    wd = w_out_o[0, D_C:].astype(BF16)
    lng, lnb = ln_g_o[0][None, :], ln_b_o[0][None, :]

    hp2 = _out_ln(yc_p, yd_p, hp1, wc, wd, lng, lnb, nb=nb, seq=lp, tm=ptm, first_valid=PAD)

    qc, kc, lfh, vc, gc, qd, kd_s, vd_s, gd = _proj_odd(hs1, wo, hgrn_gamma, nb=1, seq=ns * ts, tm=stm, first_valid=0)
    yc_s, hg_s = _hgrn(qc, kc, lfh, vc, gc, ng, state_hgrn[0], nb=ns, seq=ts, c=ts, nsub=nsub)
    kvw = KV_D * DH_D
    wkt = jnp.transpose(cache_win_k, (0, 1, 3, 4, 2)).reshape(1, ns, kvw, wb)
    wvt = jnp.transpose(cache_win_v, (0, 1, 3, 4, 2)).reshape(1, ns, kvw, wb)
    yd_s = _swa_sample(sk, qd, kd_s, vd_s, wkt, wvt, gd, nsub=nsub)
    hs2 = _out_ln(yc_s, yd_s, hs1, wc, wd, lng, lnb, nb=1, seq=ns * ts, tm=stm, first_valid=0)

    kd4 = kd.reshape(nb, lp, KV_D, DH_D)
    vd4 = vd.reshape(nb, lp, KV_D, DH_D)
    new_wk_p = kd4[:, -wb:][None]
    new_wv_p = vd4[:, -wb:][None]
    new_wk_s = jnp.concatenate([cache_win_k[0], kd_s.reshape(ns, ts, KV_D, DH_D)], axis=1)[:, -wb:][None]
    new_wv_s = jnp.concatenate([cache_win_v[0], vd_s.reshape(ns, ts, KV_D, DH_D)], axis=1)[:, -wb:][None]

    y_prompt = hp2.reshape(nb, lp, D_MODEL)[:, BLOCK:]
    y_sample = hs2.reshape(ns, ts, D_MODEL)
    return (y_prompt, y_sample, new_pool_p, new_pool_s, new_k_p, new_v_p, new_lf_p, new_k_s, new_v_s,
            new_lf_s, hg_p[None], hg_s[None], new_wk_p, new_wv_p, new_wk_s, new_wv_s)
```

```python
import functools

import numpy as np
import jax
import jax.numpy as jnp
from jax import lax
from jax.experimental import pallas as pl
from jax.experimental.pallas import tpu as pltpu

F32 = jnp.float32
BF16 = jnp.bfloat16

D_MODEL = 1024
DEPTH = 2
N_META = 16
BLOCK = 128
PAD = BLOCK - N_META
D_A = D_MODEL // 2
POOL_WINDOWS = (2, 4, 8, 16)
GC = D_A // len(POOL_WINDOWS)
W_MAX = 16
POOL_BUF = W_MAX - 1
H_B = D_MODEL // 128
DH_B = 64
D_B = H_B * DH_B
H_C = D_MODEL // 256
DK_C = 128
DV_C = 128
HK_C = H_C * DK_C
D_C = H_C * DV_C
HGRN_CHUNK = 64
H_D = D_MODEL // 128
KV_D = H_D // 4
G_D = H_D // KV_D
DH_D = 64
D_D = H_D * DH_D
WINDOW = 128
ALPHA = (2 * DEPTH) ** 0.25
LN_EPS = 1e-5
NEG = -1e30

LANES = 128
SUBLANES = 8
VMEM_LIMIT = 48 * 1024 * 1024

FOX_PAGES_PER_STEP = 16
FOX_SLOTS = 3
FOX_KEY_VARIANTS = 8
SEQS_PER_STEP = 8

_NT = (((1,), (1,)), ((), ()))


def _silu(x):
    return x * (1.0 / (1.0 + jnp.exp(-x)))


def _split3(x):
    hi = x.astype(BF16).astype(F32)
    r = x - hi
    mid = r.astype(BF16).astype(F32)
    return hi, mid, r - mid


def _dot(a, b):
    return jnp.dot(a, b, preferred_element_type=F32)


def _dot_nt(a, b):
    return lax.dot_general(a, b, _NT, preferred_element_type=F32)


def _dot3(m, x):
    hi, mid, lo = _split3(x)
    return _dot(m, hi) + _dot(m, mid) + _dot(m, lo)


def _dot3_rhs(x, m):
    hi, mid, lo = _split3(x)
    return _dot(hi, m) + _dot(mid, m) + _dot(lo, m)


def _params(sem):
    return pltpu.CompilerParams(dimension_semantics=sem, vmem_limit_bytes=VMEM_LIMIT)


def _proj_even_kernel(x_ref, w_ref, wf_ref, bf_ref, u_ref, ga_ref, q_ref, k_ref, v_ref, gb_ref,
                      lf_ref, *, tm, first_valid):
    xb = x_ref[...].astype(BF16)
    for i, o_ref in enumerate((u_ref, ga_ref, q_ref, k_ref, v_ref, gb_ref)):
        o_ref[...] = _dot(xb, w_ref[:, i * 512:(i + 1) * 512])
    z = _dot(xb, wf_ref[...]) + bf_ref[...]
    lf = jnp.minimum(z, 0.0) - jnp.log1p(jnp.exp(-jnp.abs(z)))
    pos = pl.program_id(1) * tm + lax.broadcasted_iota(jnp.int32, lf.shape, 0)
    lane = lax.broadcasted_iota(jnp.int32, lf.shape, 1)
    lf_ref[...] = jnp.where((pos >= first_valid) & (lane < H_B), lf, 0.0)


def _proj_even(x, w6, wf, bf, *, nb, seq, tm, first_valid):
    nt = seq // tm
    rows = nb * seq
    row_spec = lambda c: pl.BlockSpec((tm, c), lambda b, j: (b * nt + j, 0))
    full = lambda a: pl.BlockSpec(a.shape, lambda b, j: (0,) * a.ndim)
    outs = [jax.ShapeDtypeStruct((rows, 512), F32)] * 6 + [jax.ShapeDtypeStruct((rows, LANES), F32)]
    return pl.pallas_call(
        functools.partial(_proj_even_kernel, tm=tm, first_valid=first_valid),
        grid=(nb, nt),
        in_specs=[row_spec(D_MODEL), full(w6), full(wf), full(bf)],
        out_specs=[row_spec(512)] * 6 + [row_spec(LANES)],
        out_shape=outs,
        compiler_params=_params(("parallel", "parallel")),
        name="proj_even",
    )(x, w6, wf, bf)


def _proj_odd_kernel(x_ref, w_ref, gam_ref, qc_ref, kc_ref, lf_ref, vc_ref, gc_ref, qd_ref, kd_ref,
                     vd_ref, gd_ref, *, tm, first_valid):
    xb = x_ref[...].astype(BF16)
    g0 = gam_ref[0:1, :]
    g1 = gam_ref[1:2, :]
    mx = jnp.maximum(g0, g1)
    e0 = jnp.exp(g0 - mx)
    e1 = jnp.exp(g1 - mx)
    p0 = e0 / (e0 + e1)
    p1 = e1 / (e0 + e1)
    lb = (p0 + p1) - p0
    pos = pl.program_id(1) * tm + lax.broadcasted_iota(jnp.int32, (tm, 1), 0)
    valid = pos >= first_valid
    qc_ref[...] = _dot(xb, w_ref[:, 0:512])
    fc = _dot(xb, w_ref[:, 512:1024])
    f = lb + (1.0 - lb) * (1.0 / (1.0 + jnp.exp(-fc)))
    kc_ref[...] = jnp.where(valid, 1.0 - f, 0.0)
    lf_ref[...] = jnp.where(valid, jnp.log(f), 0.0)
    vc_ref[...] = _dot(xb, w_ref[:, 1024:1536])
    gc_ref[...] = _dot(xb, w_ref[:, 1536:2048])
    qd_ref[...] = _dot(xb, w_ref[:, 2048:2560])
    kd_ref[...] = _dot(xb, w_ref[:, 2560:2688])
    vd_ref[...] = _dot(xb, w_ref[:, 2688:2816])
    gd_ref[...] = _dot(xb, w_ref[:, 2816:3328])


def _proj_odd(x, w, gam, *, nb, seq, tm, first_valid):
    nt = seq // tm
    rows = nb * seq
    row_spec = lambda c: pl.BlockSpec((tm, c), lambda b, j: (b * nt + j, 0))
    full = lambda a: pl.BlockSpec(a.shape, lambda b, j: (0,) * a.ndim)
    widths = (512, 512, 512, 512, 512, 512, 128, 128, 512)
    return pl.pallas_call(
        functools.partial(_proj_odd_kernel, tm=tm, first_valid=first_valid),
        grid=(nb, nt),
        in_specs=[row_spec(D_MODEL), full(w), full(gam)],
        out_specs=[row_spec(c) for c in widths],
        out_shape=[jax.ShapeDtypeStruct((rows, c), F32) for c in widths],
        compiler_params=_params(("parallel", "parallel")),
        name="proj_odd",
    )(x, w, gam)


def _out_ln_kernel(a_ref, b_ref, x_ref, wa_ref, wb_ref, g_ref, bb_ref, o_ref, *, tm, first_valid):
    acc = _dot(a_ref[...].astype(BF16), wa_ref[...]) + _dot(b_ref[...].astype(BF16), wb_ref[...])
    y = ALPHA * x_ref[...] + acc
    mu = jnp.mean(y, axis=-1, keepdims=True)
    yc = y - mu
    var = jnp.mean(yc * yc, axis=-1, keepdims=True)
    o = yc * lax.rsqrt(var + LN_EPS) * g_ref[...] + bb_ref[...]
    pos = pl.program_id(1) * tm + lax.broadcasted_iota(jnp.int32, (tm, 1), 0)
    o_ref[...] = jnp.where(pos >= first_valid, o, 0.0)


def _out_ln(a, b, x, wa, wb, g, bb, *, nb, seq, tm, first_valid):
    nt = seq // tm
    row_spec = lambda c: pl.BlockSpec((tm, c), lambda i, j: (i * nt + j, 0))
    full = lambda t: pl.BlockSpec(t.shape, lambda i, j: (0,) * t.ndim)
    return pl.pallas_call(
        functools.partial(_out_ln_kernel, tm=tm, first_valid=first_valid),
        grid=(nb, nt),
        in_specs=[row_spec(512), row_spec(512), row_spec(D_MODEL), full(wa), full(wb), full(g), full(bb)],
        out_specs=row_spec(D_MODEL),
        out_shape=jax.ShapeDtypeStruct(x.shape, F32),
        compiler_params=_params(("parallel", "parallel")),
        name="out_ln",
    )(a, b, x, wa, wb, g, bb)


def _pool_kernel(u_ref, g_ref, wp_ref, sc_ref, y_ref, buf_ref, *, tr, nsub, first_valid):
    j = pl.program_id(1)
    pos = lax.broadcasted_iota(jnp.int32, (tr, 1), 0)
    if nsub == 1:
        pos = pos + j * tr
    for q in range(nsub):
        rs = slice(q * tr, (q + 1) * tr)
        if nsub == 1:
            @pl.when(j == 0)
            def _():
                buf_ref[0:W_MAX, :] = jnp.zeros((W_MAX, D_A), F32)

            @pl.when(j > 0)
            def _():
                buf_ref[0:W_MAX, :] = buf_ref[tr:tr + W_MAX, :]
        else:
            buf_ref[0:W_MAX, :] = jnp.zeros((W_MAX, D_A), F32)
        uf = jnp.where(pos >= first_valid, u_ref[rs, :], 0.0)
        buf_ref[W_MAX:W_MAX + tr, :] = uf
        for g, w in enumerate(POOL_WINDOWS):
            cs = slice(g * GC, (g + 1) * GC)
            s = buf_ref[W_MAX:W_MAX + tr, cs]
            for i in range(1, w):
                s = s + buf_ref[W_MAX - i:W_MAX - i + tr, cs]
            cnt = jnp.clip(pos - first_valid + 1, 1, w).astype(F32)
            d = s / cnt - uf[:, cs]
            y = _dot(d, wp_ref[g]) * sc_ref[:, cs]
            y_ref[rs, cs] = (y * _silu(g_ref[rs, cs])).astype(y_ref.dtype)


def _pool(u, g, wp, sc, *, nb, seq, tr, first_valid, nsub=1, ydt=F32):
    nt = seq // tr
    assert nsub == 1 or (nt == 1 and nb % nsub == 0)
    nb = nb // nsub
    row_spec = pl.BlockSpec((nsub * tr, D_A), lambda b, j: (b * nt + j, 0))
    return pl.pallas_call(
        functools.partial(_pool_kernel, tr=tr, nsub=nsub, first_valid=first_valid),
        grid=(nb, nt),
        in_specs=[row_spec, row_spec,
                  pl.BlockSpec(wp.shape, lambda b, j: (0, 0, 0)),
                  pl.BlockSpec(sc.shape, lambda b, j: (0, 0))],
        out_specs=row_spec,
        out_shape=jax.ShapeDtypeStruct(u.shape, ydt),
        scratch_shapes=[pltpu.VMEM((tr + W_MAX, D_A), F32)],
        compiler_params=_params(("parallel", "arbitrary")),
        name="pool",
    )(u, g, wp, sc)


def _fcum_kernel(lf_ref, aq_ref, ak_ref, *, nt):
    t = lax.broadcasted_iota(jnp.int32, (BLOCK, BLOCK), 0)
    lane = lax.broadcasted_iota(jnp.int32, (BLOCK, BLOCK), 1)
    tri = jnp.where(t >= lane, 1.0, 0.0).astype(F32)
    carry = jnp.zeros((1, LANES), F32)
    for j in range(nt):
        rs = slice(j * BLOCK, (j + 1) * BLOCK)
        f = _dot3(tri, lf_ref[rs, :]) + carry
        carry = f[BLOCK - 1:BLOCK, :]
        fh, fm, fl = _split3(f)
        f3 = fh + pltpu.roll(fm, H_B, 1) + pltpu.roll(fl, 2 * H_B, 1)
        aq_ref[rs, :] = jnp.where((lane >= 3 * H_B) & (lane < 6 * H_B), 1.0, f3)
        ak_ref[rs, :] = jnp.where(lane < 3 * H_B, 1.0, -pltpu.roll(f3, 3 * H_B, 1))


def _fcum(lf, *, nb, seq):
    spec = pl.BlockSpec((seq, LANES), lambda b: (b, 0))
    sds = jax.ShapeDtypeStruct(lf.shape, F32)
    return pl.pallas_call(
        functools.partial(_fcum_kernel, nt=seq // BLOCK),
        grid=(nb,),
        in_specs=[spec],
        out_specs=[spec, spec],
        out_shape=[sds, sds],
        compiler_params=_params(("parallel",)),
        name="fcum",
    )(lf)


def _fox_prompt_kernel(q_ref, k_ref, v_ref, aq_ref, ak_ref, gb_ref, y_ref, *, first_valid, klens):
    i = pl.program_id(1)
    lane = lax.broadcasted_iota(jnp.int32, (BLOCK, LANES), 1)
    low = lane < DH_B
    row = lax.broadcasted_iota(jnp.int32, (2 * BLOCK, 1), 0)
    qpos = i * BLOCK + (row & (BLOCK - 1))

    def attend(lk):
        kpos = lax.broadcasted_iota(jnp.int32, (1, lk), 1)
        mask = (kpos <= qpos) & (kpos >= first_valid)
        aq = aq_ref[...]
        ak = ak_ref[0:lk, :]
        for j in range(H_B // 2):
            cs = slice(j * LANES, (j + 1) * LANES)
            qp = q_ref[:, cs] * (DH_B ** -0.5)
            a0 = jnp.where(((lane & (H_B - 1)) == 2 * j) & (lane < 6 * H_B), aq, 0.0)
            a1 = jnp.where(((lane & (H_B - 1)) == 2 * j + 1) & (lane < 6 * H_B), aq, 0.0)
            qs = jnp.concatenate([
                jnp.concatenate([jnp.where(low, qp, 0.0), a0], axis=1),
                jnp.concatenate([jnp.where(low, 0.0, qp), a1], axis=1)], axis=0)
            ks = jnp.concatenate([k_ref[0:lk, cs], ak], axis=1)
            s = jnp.where(mask, _dot_nt(qs, ks), NEG)
            m = jnp.max(s, axis=-1, keepdims=True)
            p = jnp.exp(s - m)
            l = jnp.sum(p, axis=-1, keepdims=True)
            r = _dot(p, v_ref[0:lk, cs]) / l
            o = jnp.where(low, r[:BLOCK], r[BLOCK:])
            y_ref[:, cs] = (o * _silu(gb_ref[:, cs])).astype(y_ref.dtype)

    lo = 0
    for hi in klens:
        @pl.when((i >= lo) & (i < hi))
        def _(hi=hi):
            attend(hi * BLOCK)
        lo = hi


def _fox_prompt(q, k, v, aq, ak, gb, *, nb, seq, first_valid, ydt=F32):
    nt = seq // BLOCK
    nvar = min(FOX_KEY_VARIANTS, nt)
    klens = tuple(-(-nt * (x + 1) // nvar) for x in range(nvar))
    blk = lambda c: pl.BlockSpec((BLOCK, c), lambda b, i: (b * nt + i, 0))
    whole = lambda c: pl.BlockSpec((seq, c), lambda b, i: (b, 0))
    return pl.pallas_call(
        functools.partial(_fox_prompt_kernel, first_valid=first_valid, klens=klens),
        grid=(nb, nt),
        in_specs=[blk(D_B), whole(D_B), whole(D_B), blk(LANES), whole(LANES), blk(D_B)],
        out_specs=blk(D_B),
        out_shape=jax.ShapeDtypeStruct(q.shape, ydt),
        compiler_params=_params(("parallel", "parallel")),
        name="fox_prompt",
    )(q, k, v, aq, ak, gb)


def _fox_page_copies(pt_ref, kt_hbm, vt_hbm, lf_hbm, kbuf, vbuf, lbuf, sem, seq, step, slot, *, gp, npages):
    copies = []
    for g in range(gp):
        page = pt_ref[seq, npages - (step + 1) * gp + g]
        copies.append(pltpu.make_async_copy(kt_hbm.at[0, page], kbuf.at[slot, g], sem.at[slot]))
        copies.append(pltpu.make_async_copy(vt_hbm.at[0, page], vbuf.at[slot, g], sem.at[slot]))
        copies.append(pltpu.make_async_copy(lf_hbm.at[0, page], lbuf.at[slot, g], sem.at[slot]))
    return copies


def _fox_sample_kernel(pt_ref, *refs, gp, npages, side):
    q_ref, kn_ref, vn_ref, lft_ref, gb_ref, ut_ref, kt_hbm, vt_hbm, lf_hbm = refs[:9]
    n_in = 9 + (14 if side else 0)
    n_out = 1 + (3 if side else 0)
    y_ref = refs[n_in]
    qbd_ref, m_ref, l_ref, acc_ref, suf_ref, kbuf, vbuf, lbuf, sem = refs[n_in + n_out:n_in + n_out + 9]
    seq = pl.program_id(0)
    step = pl.program_id(1)
    nseq = pl.num_programs(0)
    steps = npages // gp
    lin = seq * steps + step
    copies = functools.partial(_fox_page_copies, pt_ref, kt_hbm, vt_hbm, lf_hbm, kbuf, vbuf, lbuf, sem,
                               gp=gp, npages=npages)

    def start(cs):
        for idx, c in enumerate(cs):
            c.start(priority=idx % 2)

    @pl.when(lin == 0)
    def _():
        for d in range(FOX_SLOTS - 1):
            start(copies(0, d, d))

    ahead = step + (FOX_SLOTS - 1)
    wrap = (ahead >= steps).astype(jnp.int32)
    seq_a = seq + wrap
    step_a = ahead - wrap * steps

    @pl.when(seq_a < nseq)
    def _():
        start(copies(seq_a, step_a, lax.rem(lin + (FOX_SLOTS - 1), FOX_SLOTS)))

    if side:
        hg_in, sw_in = refs[9:16], refs[16:23]
        y_hg, s_hg, y_sw = refs[n_in + 1:n_in + 4]
        st_hg = refs[n_in + n_out + 9]
        hu = lax.div(lin, side["hg_every"])

        @pl.when((lax.rem(lin, side["hg_every"]) == 0) & (hu < side["hg_units"]))
        def _():
            j = lax.rem(hu, side["nc"])
            _hgrn_step(*hg_in, None, y_hg, s_hg, st_hg, j == 0, j == side["nc"] - 1, c=HGRN_CHUNK, nsub=1)

        su = lax.div(lin, side["sw_every"])

        @pl.when((lax.rem(lin, side["sw_every"]) == 0) & (su < side["sw_units"]))
        def _():
            _swa_prompt_step(*sw_in, y_sw, lax.rem(su, side["nt"]), first_valid=side["first_valid"])

    slot = lax.rem(lin, FOX_SLOTS)
    for c in copies(seq, step, slot):
        c.wait()

    lane512 = lax.broadcasted_iota(jnp.int32, (SUBLANES, D_B), 1)
    nrow = H_B * SUBLANES

    @pl.when(step == 0)
    def _():
        qs = q_ref[...] * (DH_B ** -0.5)
        for h in range(H_B):
            qbd_ref[h * 8:(h + 1) * 8, :] = jnp.where((lane512 >> 6) == h, qs, 0.0)
        lft = lft_ref[0]
        lane = lax.broadcasted_iota(jnp.int32, (H_B, LANES), 1)
        gcum = jnp.zeros((H_B, LANES), F32)
        for u in range(SUBLANES):
            gcum = gcum + jnp.where(lane >= u, jnp.broadcast_to(lft[:, u:u + 1], (H_B, LANES)), 0.0)
        kpad = jnp.concatenate([kn_ref[...], jnp.zeros((LANES - 8, D_B), F32)], axis=0)
        vpad = jnp.concatenate([vn_ref[...], jnp.zeros((LANES - 8, D_B), F32)], axis=0)
        s = _dot_nt(qbd_ref[...], kpad)
        bias = jnp.concatenate(
            [jnp.broadcast_to(gcum[h:h + 1, :], (8, LANES)) for h in range(H_B)], axis=0)
        t = lax.broadcasted_iota(jnp.int32, (nrow, LANES), 0) & 7
        sl = lax.broadcasted_iota(jnp.int32, (nrow, LANES), 1)
        s = jnp.where(sl <= t, s - bias, NEG)
        m = jnp.max(s, axis=-1, keepdims=True)
        p = jnp.exp(s - m)
        m_ref[...] = m
        l_ref[...] = jnp.sum(p, axis=-1, keepdims=True)
        acc_ref[...] = _dot(p, vpad)
        suf_ref[...] = jnp.zeros(suf_ref.shape, F32)

    lfs = jnp.concatenate([lbuf[slot, g] for g in range(gp)], axis=0)
    wt = _dot3_rhs(lfs, ut_ref[...])
    suf = suf_ref[...]
    biases = [None] * gp
    for g in range(gp - 1, -1, -1):
        biases[g] = wt[g * 8:(g + 1) * 8, :LANES] + suf
        suf = suf + wt[g * 8:(g + 1) * 8, LANES:]
    suf_ref[...] = suf

    qbd = qbd_ref[...]
    ss = []
    for g in range(gp):
        sg = _dot(qbd, kbuf[slot, g])
        bg = jnp.concatenate(
            [jnp.broadcast_to(biases[g][h:h + 1, :], (8, LANES)) for h in range(H_B)], axis=0)
        ss.append(sg + bg)
    m_prev = m_ref[...]
    m_cur = ss[0]
    for g in range(1, gp):
        m_cur = jnp.maximum(m_cur, ss[g])
    m_new = jnp.maximum(m_prev, jnp.max(m_cur, axis=-1, keepdims=True))
    alpha = jnp.exp(m_prev - m_new)
    psum = None
    pv = None
    for g in range(gp):
        p = jnp.exp(ss[g] - m_new)
        psum = p if psum is None else psum + p
        d = _dot_nt(p, vbuf[slot, g])
        pv = d if pv is None else pv + d
    m_ref[...] = m_new
    l_ref[...] = alpha * l_ref[...] + jnp.sum(psum, axis=-1, keepdims=True)
    acc_ref[...] = alpha * acc_ref[...] + pv

    @pl.when(step == steps - 1)
    def _():
        o = acc_ref[...] / l_ref[...]
        out = jnp.zeros((SUBLANES, D_B), F32)
        for h in range(H_B):
            out = out + jnp.where((lane512 >> 6) == h, o[h * 8:(h + 1) * 8, :], 0.0)
        y_ref[...] = out * _silu(gb_ref[...])


def _fox_sample(page_table, q, kn, vn, lft, gb, kt, vt, lfc, *, gp, hgrn_args=None, swa_args=None,
                nb=0, seq=0, first_valid=0, ydt=F32):
    n, npages = page_table.shape
    steps = npages // gp
    total = n * steps
    assert npages % gp == 0 and steps >= FOX_SLOTS - 1
    s_i = np.arange(LANES)
    ut = np.concatenate([(s_i[:, None] > s_i[None, :]).astype(np.float32),
                         np.ones((LANES, LANES), np.float32)], axis=1)
    row = lambda c: pl.BlockSpec((8, c), lambda i, s, pt: (i, 0))
    hbm = pl.BlockSpec(memory_space=pl.ANY)
    const = lambda a: pl.BlockSpec(a.shape, lambda i, s, pt: (0,) * a.ndim)
    in_specs = [row(D_B), row(D_B), row(D_B),
                pl.BlockSpec((1, H_B, LANES), lambda i, s, pt: (i, 0, 0)),
                row(D_B), const(ut), hbm, hbm, hbm]
    args = [q, kn, vn, lft, gb, jnp.asarray(ut), kt, vt, lfc]
    out_specs = [pl.BlockSpec((8, D_B), lambda i, s, pt: (i, 0))]
    out_shape = [jax.ShapeDtypeStruct(q.shape, F32)]
    nrow = H_B * SUBLANES
    scratch = [pltpu.VMEM((nrow, D_B), F32), pltpu.VMEM((nrow, 1), F32),
               pltpu.VMEM((nrow, 1), F32), pltpu.VMEM((nrow, D_B), F32),
               pltpu.VMEM((H_B, LANES), F32),
               pltpu.VMEM((FOX_SLOTS, gp, D_B, LANES), F32),
               pltpu.VMEM((FOX_SLOTS, gp, D_B, LANES), F32),
               pltpu.VMEM((FOX_SLOTS, gp, H_B, LANES), F32),
               pltpu.SemaphoreType.DMA((FOX_SLOTS,))]
    side = None
    if hgrn_args is not None:
        nc, nt = seq // HGRN_CHUNK, seq // BLOCK
        hg_units, sw_units = nb * nc, nb * nt
        assert total >= hg_units and total >= sw_units
        side = dict(hg_every=total // hg_units, hg_units=hg_units, nc=nc,
                    sw_every=total // sw_units, sw_units=sw_units, nt=nt, first_valid=first_valid)

        def unit(every, units):
            return lambda i, s, pt: jnp.minimum(lax.div(i * steps + s, every), units - 1)

        hu, su = unit(side["hg_every"], hg_units), unit(side["sw_every"], sw_units)
        hq, hk, hlf, hv, hg, hng = hgrn_args
        ssk, sq, sk_, sv, sg = swa_args
        ms = jnp.asarray(_hgrn_mstack(HGRN_CHUNK))
        hblk = pl.BlockSpec((HGRN_CHUNK, HK_C), lambda i, s, pt: (hu(i, s, pt), 0))
        st_spec = pl.BlockSpec((1, H_C, DK_C, DV_C), lambda i, s, pt: (lax.div(hu(i, s, pt), nc), 0, 0, 0))
        cur = lambda c: pl.BlockSpec((BLOCK, c), lambda i, s, pt: (su(i, s, pt), 0))

        def prev_idx(i, s, pt):
            u = su(i, s, pt)
            return u - jnp.minimum(lax.rem(u, nt), 1)

        prev = lambda c: pl.BlockSpec((BLOCK, c), lambda i, s, pt: (prev_idx(i, s, pt), 0))
        kvw = KV_D * DH_D
        in_specs += [hblk] * 5 + [const(ms), const(hng)]
        in_specs += [pl.BlockSpec(memory_space=pltpu.SMEM), cur(D_D), prev(kvw), cur(kvw), prev(kvw),
                     cur(kvw), cur(D_D)]
        args += [hq, hk, hlf, hv, hg, ms, hng, ssk, sq, sk_, sk_, sv, sv, sg]
        out_specs += [hblk, st_spec, cur(D_D)]
        out_shape += [jax.ShapeDtypeStruct(hq.shape, ydt),
                      jax.ShapeDtypeStruct((nb, H_C, DK_C, DV_C), F32),
                      jax.ShapeDtypeStruct(sq.shape, ydt)]
        scratch.append(pltpu.VMEM((H_C, DV_C, DK_C), F32))
    grid_spec = pltpu.PrefetchScalarGridSpec(
        num_scalar_prefetch=1, grid=(n, steps), in_specs=in_specs,
        out_specs=out_specs if side else out_specs[0], scratch_shapes=scratch)
    return pl.pallas_call(
        functools.partial(_fox_sample_kernel, gp=gp, npages=npages, side=side),
        grid_spec=grid_spec,
        out_shape=out_shape if side else out_shape[0],
        compiler_params=_params(("arbitrary", "arbitrary")),
        name="fox_sample",
    )(page_table, *args)


def _hgrn_levels(c):
    out = []
    m = c // 2
    while m >= 1:
        out.append(m)
        m //= 2
    return tuple(out)


def _hgrn_mstack(c):
    t = np.arange(c)[:, None]
    u = np.arange(c)[None, :]
    mats = [(u <= t)]
    for m in _hgrn_levels(c):
        r = (t // (2 * m)) * (2 * m) + m - 1
        second = (t % (2 * m)) >= m
        mats.append(np.where(second, (u > r) & (u <= t), (u > t) & (u <= r)))
    return np.concatenate(mats, axis=0).astype(np.float32)


def _hgrn_kernel(*refs, c, nsub, has_s0):
    if has_s0:
        q_ref, k_ref, lf_ref, v_ref, g_ref, m_ref, ng_ref, s0_ref, y_ref, so_ref, st_ref = refs
    else:
        q_ref, k_ref, lf_ref, v_ref, g_ref, m_ref, ng_ref, y_ref, so_ref, st_ref = refs
        s0_ref = None
    ci = pl.program_id(1)
    _hgrn_step(q_ref, k_ref, lf_ref, v_ref, g_ref, m_ref, ng_ref, s0_ref, y_ref, so_ref, st_ref,
               ci == 0, ci == pl.num_programs(1) - 1, c=c, nsub=nsub)


def _hgrn_step(q_ref, k_ref, lf_ref, v_ref, g_ref, m_ref, ng_ref, s0_ref, y_ref, so_ref, st_ref,
               first, last, *, c, nsub):
    has_s0 = s0_ref is not None
    levels = _hgrn_levels(c)
    units = [(q, h) for q in range(nsub) for h in range(H_C)]

    @pl.when(first)
    def _():
        for q, h in units:
            st_ref[q * H_C + h] = s0_ref[q, h].T if has_s0 else jnp.zeros((DV_C, DK_C), F32)

    t_col = lax.broadcasted_iota(jnp.int32, (c, 1), 0)
    t_row = lax.broadcasted_iota(jnp.int32, (c, c), 0)
    s_col = lax.broadcasted_iota(jnp.int32, (c, c), 1)
    rpad = max(LANES - c, 0)
    mstack = m_ref[...]
    sub = {}
    for q in range(nsub):
        rs = slice(q * c, (q + 1) * c)
        e = _dot3(mstack, lf_ref[rs, :])
        b = e[0:c]
        qv, kv = q_ref[rs, :], k_ref[rs, :]
        bl = b[c - 1:c, :]
        qx, kx = [], []
        for li, m in enumerate(levels):
            x = jnp.exp(e[(li + 1) * c:(li + 2) * c])
            second = (t_col & (2 * m - 1)) >= m
            qx.append(jnp.where(second, qv * x, 0.0))
            kx.append(jnp.where(second, 0.0, kv * x))
        sub[q] = dict(rs=rs, qb=qv * jnp.exp(b), qk=qv * kv, kp=kv * jnp.exp(bl - b), dec=jnp.exp(bl),
                      qx=qx, kx=kx, v=v_ref[rs, :])
    o_inter, a_mat = {}, {}
    for q, h in units:
        cs = slice(h * DK_C, (h + 1) * DK_C)
        d = sub[q]
        o_inter[q, h] = _dot_nt(d["qb"][:, cs], st_ref[q * H_C + h])
        a = jnp.where(t_row == s_col, jnp.sum(d["qk"][:, cs], axis=-1, keepdims=True), 0.0)
        for li, m in enumerate(levels):
            sh = (2 * m).bit_length() - 1
            am = _dot_nt(d["qx"][li][:, cs], d["kx"][li][:, cs])
            a = a + jnp.where((t_row >> sh) == (s_col >> sh), am, 0.0)
        a_mat[q, h] = a
    for q, h in units:
        cs = slice(h * DK_C, (h + 1) * DK_C)
        d = sub[q]
        v = d["v"][:, cs]
        kp = d["kp"][:, cs]
        o = o_inter[q, h] + _dot(a_mat[q, h], v)
        if rpad:
            v = jnp.concatenate([v, jnp.zeros((rpad, DV_C), F32)], axis=0)
            kp = jnp.concatenate([kp, jnp.zeros((rpad, DK_C), F32)], axis=0)
        st_ref[q * H_C + h] = st_ref[q * H_C + h] * d["dec"][:, cs] + _dot(v.T, kp)
        y = o * lax.rsqrt(jnp.mean(o * o, axis=-1, keepdims=True) + LN_EPS) * ng_ref[...]
        y_ref[d["rs"], cs] = (y * _silu(g_ref[d["rs"], cs])).astype(y_ref.dtype)

    @pl.when(last)
    def _():
        for q, h in units:
            so_ref[q, h] = st_ref[q * H_C + h].T


def _hgrn(q, k, lf, v, g, ng, s0, *, nb, seq, c, nsub=1, ydt=F32):
    nc = seq // c
    assert nsub == 1 or (nc == 1 and nb % nsub == 0)
    ms = jnp.asarray(_hgrn_mstack(c))
    blk = pl.BlockSpec((nsub * c, HK_C), lambda b, j: (b * nc + j, 0))
    st_spec = pl.BlockSpec((nsub, H_C, DK_C, DV_C), lambda b, j: (b, 0, 0, 0))
    in_specs = [blk, blk, blk, blk, blk,
                pl.BlockSpec(ms.shape, lambda b, j: (0, 0)),
                pl.BlockSpec(ng.shape, lambda b, j: (0, 0))]
    args = [q, k, lf, v, g, ms, ng]
    if s0 is not None:
        in_specs.append(st_spec)
        args.append(s0)
    return pl.pallas_call(
        functools.partial(_hgrn_kernel, c=c, nsub=nsub, has_s0=s0 is not None),
        grid=(nb // nsub, nc),
        in_specs=in_specs,
        out_specs=[blk, st_spec],
        out_shape=[jax.ShapeDtypeStruct(q.shape, ydt),
                   jax.ShapeDtypeStruct((nb, H_C, DK_C, DV_C), F32)],
        scratch_shapes=[pltpu.VMEM((nsub * H_C, DV_C, DK_C), F32)],
        compiler_params=_params(("parallel", "arbitrary")),
        name="hgrn",
    )(*args)


def _swa_prompt_kernel(sk_ref, q_ref, kp_ref, kc_ref, vp_ref, vc_ref, g_ref, y_ref, *, first_valid):
    _swa_prompt_step(sk_ref, q_ref, kp_ref, kc_ref, vp_ref, vc_ref, g_ref, y_ref, pl.program_id(1),
                     first_valid=first_valid)


def _swa_prompt_step(sk_ref, q_ref, kp_ref, kc_ref, vp_ref, vc_ref, g_ref, y_ref, i, *, first_valid):
    lane = lax.broadcasted_iota(jnp.int32, (BLOCK, LANES), 1)
    low = lane < DH_D
    kk = jnp.concatenate([kp_ref[...], kc_ref[...]], axis=0)
    vv = jnp.concatenate([vp_ref[...], vc_ref[...]], axis=0)
    kk_sw = pltpu.roll(kk, DH_D, 1)
    vv_sw = pltpu.roll(vv, DH_D, 1)
    kj = lax.broadcasted_iota(jnp.int32, (BLOCK, 2 * BLOCK), 1)
    tq = lax.broadcasted_iota(jnp.int32, (BLOCK, 2 * BLOCK), 0)
    dist = tq + BLOCK - kj
    kpos = i * BLOCK - BLOCK + kj
    mask = (dist >= 0) & (dist <= WINDOW) & (kpos >= first_valid)
    distf = dist.astype(F32)
    ss = []
    for h in range(H_D):
        j, e, kvh = h // 2, h % 2, h // G_D
        qp = q_ref[:, j * LANES:(j + 1) * LANES] * (DH_D ** -0.5)
        qm = jnp.where(low, qp, 0.0) if e == 0 else jnp.where(low, 0.0, qp)
        s = _dot_nt(qm, kk if e == kvh else kk_sw) - (2.0 ** -(h + 1)) * distf
        ss.append(jnp.where(mask, s, NEG))
    ps, dens = [], []
    for h in range(H_D):
        sink = sk_ref[h]
        m = jnp.maximum(jnp.max(ss[h], axis=-1, keepdims=True), sink)
        p = jnp.exp(ss[h] - m)
        ps.append(p)
        dens.append(jnp.sum(p, axis=-1, keepdims=True) + jnp.exp(sink - m))
    outs = [_dot(ps[h], vv if h % 2 == h // G_D else vv_sw) / dens[h] for h in range(H_D)]
    for j in range(H_D // 2):
        cs = slice(j * LANES, (j + 1) * LANES)
        y_ref[:, cs] = (jnp.where(low, outs[2 * j], outs[2 * j + 1]) * _silu(g_ref[:, cs])).astype(y_ref.dtype)


def _swa_prompt(sinks, q, k, v, g, *, nb, seq, first_valid, ydt=F32):
    nt = seq // BLOCK
    cur = lambda c: pl.BlockSpec((BLOCK, c), lambda b, i: (b * nt + i, 0))
    prev = lambda c: pl.BlockSpec((BLOCK, c), lambda b, i: (b * nt + jnp.maximum(i - 1, 0), 0))
    kvw = KV_D * DH_D
    return pl.pallas_call(
        functools.partial(_swa_prompt_kernel, first_valid=first_valid),
        grid=(nb, nt),
        in_specs=[pl.BlockSpec(memory_space=pltpu.SMEM),
                  cur(D_D), prev(kvw), cur(kvw), prev(kvw), cur(kvw), cur(D_D)],
        out_specs=cur(D_D),
        out_shape=jax.ShapeDtypeStruct(q.shape, ydt),
        compiler_params=_params(("parallel", "parallel")),
        name="swa_prompt",
    )(sinks, q, k, k, v, v, g)


def _swa_sample_kernel(sk_ref, q_ref, kn_ref, vn_ref, kt_ref, vt_ref, g_ref, y_ref, *, nsub):
    nrow = H_D * SUBLANES
    lane8 = lax.broadcasted_iota(jnp.int32, (SUBLANES, LANES), 1)
    rows = lax.broadcasted_iota(jnp.int32, (nrow, 1), 0)
    hrow = rows >> 3
    t = rows & 7
    slope = jnp.zeros((nrow, 1), F32)
    sink = jnp.zeros((nrow, 1), F32)
    for h in range(H_D):
        slope = jnp.where(hrow == h, 2.0 ** -(h + 1), slope)
        sink = jnp.where(hrow == h, sk_ref[h], sink)
    kj = lax.broadcasted_iota(jnp.int32, (nrow, LANES), 1)
    d_old = t + WINDOW - kj
    d_new = t - kj
    for u in range(nsub):
        rs = slice(u * 8, (u + 1) * 8)
        blocks = []
        for h in range(H_D):
            j, e, kvh = h // 2, h % 2, h // G_D
            qp = q_ref[rs, j * LANES:(j + 1) * LANES] * (DH_D ** -0.5)
            src = qp if e == kvh else pltpu.roll(qp, DH_D, 1)
            blocks.append(jnp.where((lane8 >> 6) == kvh, src, 0.0))
        qbd = jnp.concatenate(blocks, axis=0)
        kpad = jnp.concatenate([kn_ref[rs, :], jnp.zeros((LANES - 8, LANES), F32)], axis=0)
        vpad = jnp.concatenate([vn_ref[rs, :], jnp.zeros((LANES - 8, LANES), F32)], axis=0)
        s_old = _dot(qbd, kt_ref[0, u]) - slope * d_old.astype(F32)
        s_old = jnp.where(d_old <= WINDOW, s_old, NEG)
        s_new = _dot_nt(qbd, kpad) - slope * d_new.astype(F32)
        s_new = jnp.where(d_new >= 0, s_new, NEG)
        m = jnp.maximum(jnp.maximum(jnp.max(s_old, axis=-1, keepdims=True),
                                    jnp.max(s_new, axis=-1, keepdims=True)), sink)
        p_old = jnp.exp(s_old - m)
        p_new = jnp.exp(s_new - m)
        den = (jnp.sum(p_old, axis=-1, keepdims=True) + jnp.sum(p_new, axis=-1, keepdims=True)
               + jnp.exp(sink - m))
        o = (_dot_nt(p_old, vt_ref[0, u]) + _dot(p_new, vpad)) / den
        o_sw = pltpu.roll(o, DH_D, 1)
        for j in range(H_D // 2):
            parts = []
            for e in range(2):
                h = 2 * j + e
                parts.append((o if e == h // G_D else o_sw)[h * 8:(h + 1) * 8, :])
            cs = slice(j * LANES, (j + 1) * LANES)
            y_ref[rs, cs] = jnp.where(lane8 < DH_D, parts[0], parts[1]) * _silu(g_ref[rs, cs])


def _swa_sample(sinks, q, kn, vn, kt, vt, g, *, nsub):
    n = q.shape[0] // 8
    assert n % nsub == 0
    row = lambda c: pl.BlockSpec((nsub * 8, c), lambda i: (i, 0))
    buf = pl.BlockSpec((1, nsub, LANES, LANES), lambda i: (0, i, 0, 0))
    kvw = KV_D * DH_D
    return pl.pallas_call(
        functools.partial(_swa_sample_kernel, nsub=nsub),
        grid=(n // nsub,),
        in_specs=[pl.BlockSpec(memory_space=pltpu.SMEM), row(D_D), row(kvw), row(kvw), buf, buf, row(D_D)],
        out_specs=row(D_D),
        out_shape=jax.ShapeDtypeStruct(q.shape, F32),
        compiler_params=_params(("parallel",)),
        name="swa_sample",
    )(sinks, q, kn, vn, kt, vt, g)


def kernel(x_prompt, x_sample, cache_pool, cache_k, cache_v, cache_logf, state_hgrn, cache_win_k,
           cache_win_v, page_table, meta_tokens, w_in_e, b_f, w_pool, pool_scale, w_out_e, ln_g_e,
           ln_b_e, w_in_o, hgrn_gamma, hgrn_norm_g, sinks, w_out_o, ln_g_o, ln_b_o):
    assert w_in_e.shape[0] == 1 and w_in_o.shape[0] == 1 and hgrn_gamma.shape[0] == DEPTH
    nb, sp, _ = x_prompt.shape
    ns, ts, _ = x_sample.shape
    assert ts == 8 and sp % BLOCK == 0
    lp = sp + BLOCK
    n_pool = cache_k.shape[1]
    wb = cache_win_k.shape[2]
    assert wb == WINDOW and cache_k.shape[2] == LANES
    ptm = lp // 4
    nsub = max(d for d in range(1, SEQS_PER_STEP + 1) if ns % d == 0)
    stm = min(256, ns * ts)
    assert lp % (4 * SUBLANES) == 0 and (ns * ts) % stm == 0

    hp = jnp.concatenate([jnp.zeros((nb, PAD, D_MODEL), F32),
                          jnp.broadcast_to(meta_tokens, (nb, N_META, D_MODEL)), x_prompt], axis=1)
    hp = hp.reshape(nb * lp, D_MODEL)
    hs = x_sample.reshape(ns * ts, D_MODEL)

    we = w_in_e[0]
    w6 = we[:, :6 * 512].astype(BF16)
    wf = jnp.pad(we[:, 6 * 512:], ((0, 0), (0, LANES - H_B))).astype(BF16)
    bf = jnp.pad(b_f[0][None, :], ((0, 0), (0, LANES - H_B)))
    wa = w_out_e[0, :D_A].astype(BF16)
    wbm = w_out_e[0, D_A:].astype(BF16)
    sc = pool_scale[0][None, :]
    lng, lnb = ln_g_e[0][None, :], ln_b_e[0][None, :]

    u_p, ga_p, q_p, k_p, v_p, gb_p, lf_p = _proj_even(hp, w6, wf, bf, nb=nb, seq=lp, tm=ptm, first_valid=PAD)
    ya_p = _pool(u_p, ga_p, w_pool[0], sc, nb=nb, seq=lp, tr=ptm, first_valid=PAD, ydt=BF16)
    aq, ak = _fcum(lf_p, nb=nb, seq=lp)
    yb_p = _fox_prompt(q_p, k_p, v_p, aq, ak, gb_p, nb=nb, seq=lp, first_valid=PAD, ydt=BF16)
    hp1 = _out_ln(ya_p, yb_p, hp, wa, wbm, lng, lnb, nb=nb, seq=lp, tm=ptm, first_valid=PAD)

    wo = w_in_o[0].astype(BF16)
    ng = hgrn_norm_g[0][None, :]
    sk = sinks[0]
    qc, kc, lfh, vc, gc, qd, kd, vd, gd = _proj_odd(hp1, wo, hgrn_gamma, nb=nb, seq=lp, tm=ptm, first_valid=PAD)

    u_s, ga_s, q_s, k_s, v_s, gb_s, lf_s = _proj_even(hs, w6, wf, bf, nb=1, seq=ns * ts, tm=stm, first_valid=0)
    hist = POOL_BUF + ts + 1
    u3 = u_s.reshape(ns, ts, D_A)
    u_ext = jnp.concatenate([cache_pool[0], u3, jnp.zeros((ns, 1, D_A), F32)], axis=1)
    g_ext = jnp.concatenate([jnp.zeros((ns, POOL_BUF, D_A), F32), ga_s.reshape(ns, ts, D_A),
                             jnp.zeros((ns, 1, D_A), F32)], axis=1)
    ya_s = _pool(u_ext.reshape(ns * hist, D_A), g_ext.reshape(ns * hist, D_A), w_pool[0], sc,
                 nb=ns, seq=hist, tr=hist, first_valid=0, nsub=nsub)
    ya_s = ya_s.reshape(ns, hist, D_A)[:, POOL_BUF:POOL_BUF + ts].reshape(ns * ts, D_A)
    kt = jnp.transpose(cache_k, (0, 1, 3, 4, 2)).reshape(1, n_pool, D_B, LANES)
    vt = jnp.transpose(cache_v, (0, 1, 3, 4, 2)).reshape(1, n_pool, D_B, LANES)
    lfc = jnp.transpose(cache_logf, (0, 1, 3, 2))
    lft = jnp.transpose(lf_s[:, :H_B].reshape(ns, ts, H_B), (0, 2, 1))
    lft = jnp.pad(lft, ((0, 0), (0, 0), (0, LANES - ts)))
    fox_steps = ns * (page_table.shape[1] // FOX_PAGES_PER_STEP)
    ride = fox_steps >= nb * (lp // HGRN_CHUNK)
    if ride:
        yb_s, yc_p, hg_p, yd_p = _fox_sample(
            page_table, q_s, k_s, v_s, lft, gb_s, kt, vt, lfc, gp=FOX_PAGES_PER_STEP,
            hgrn_args=(qc, kc, lfh, vc, gc, ng), swa_args=(sk, qd, kd, vd, gd),
            nb=nb, seq=lp, first_valid=PAD, ydt=BF16)
    else:
        yb_s = _fox_sample(page_table, q_s, k_s, v_s, lft, gb_s, kt, vt, lfc, gp=FOX_PAGES_PER_STEP)
        yc_p, hg_p = _hgrn(qc, kc, lfh, vc, gc, ng, None, nb=nb, seq=lp, c=HGRN_CHUNK, ydt=BF16)
        yd_p = _swa_prompt(sk, qd, kd, vd, gd, nb=nb, seq=lp, first_valid=PAD, ydt=BF16)
    hs1 = _out_ln(ya_s, yb_s, hs, wa, wbm, lng, lnb, nb=1, seq=ns * ts, tm=stm, first_valid=0)

    new_pool_p = u_p.reshape(nb, lp, D_A)[:, -POOL_BUF:][None]
    new_pool_s = u_ext[:, ts:ts + POOL_BUF][None]
    new_k_p = k_p.reshape(nb, lp, H_B, DH_B)[:, PAD:][None]
    new_v_p = v_p.reshape(nb, lp, H_B, DH_B)[:, PAD:][None]
    new_lf_p = lf_p.reshape(nb, lp, LANES)[:, PAD:, :H_B][None]
    new_k_s = k_s.reshape(ns, ts, H_B, DH_B)[None]
    new_v_s = v_s.reshape(ns, ts, H_B, DH_B)[None]
    new_lf_s = lf_s[:, :H_B].reshape(ns, ts, H_B)[None]

    wc = w_out_o[0, :D_C].astype(BF16)
    wd = w_out_o[0, D_C:].astype(BF16)
    lng, lnb = ln_g_o[0][None, :], ln_b_o[0][None, :]

    hp2 = _out_ln(yc_p, yd_p, hp1, wc, wd, lng, lnb, nb=nb, seq=lp, tm=ptm, first_valid=PAD)

    qc, kc, lfh, vc, gc, qd, kd_s, vd_s, gd = _proj_odd(hs1, wo, hgrn_gamma, nb=1, seq=ns * ts, tm=stm, first_valid=0)
    yc_s, hg_s = _hgrn(qc, kc, lfh, vc, gc, ng, state_hgrn[0], nb=ns, seq=ts, c=ts, nsub=nsub)
    kvw = KV_D * DH_D
    wkt = jnp.transpose(cache_win_k, (0, 1, 3, 4, 2)).reshape(1, ns, kvw, wb)
    wvt = jnp.transpose(cache_win_v, (0, 1, 3, 4, 2)).reshape(1, ns, kvw, wb)
    yd_s = _swa_sample(sk, qd, kd_s, vd_s, wkt, wvt, gd, nsub=nsub)
    hs2 = _out_ln(yc_s, yd_s, hs1, wc, wd, lng, lnb, nb=1, seq=ns * ts, tm=stm, first_valid=0)

    kd4 = kd.reshape(nb, lp, KV_D, DH_D)
    vd4 = vd.reshape(nb, lp, KV_D, DH_D)
    new_wk_p = kd4[:, -wb:][None]
    new_wv_p = vd4[:, -wb:][None]
    new_wk_s = jnp.concatenate([cache_win_k[0], kd_s.reshape(ns, ts, KV_D, DH_D)], axis=1)[:, -wb:][None]
    new_wv_s = jnp.concatenate([cache_win_v[0], vd_s.reshape(ns, ts, KV_D, DH_D)], axis=1)[:, -wb:][None]

    y_prompt = hp2.reshape(nb, lp, D_MODEL)[:, BLOCK:]
    y_sample = hs2.reshape(ns, ts, D_MODEL)
    return (y_prompt, y_sample, new_pool_p, new_pool_s, new_k_p, new_v_p, new_lf_p, new_k_s, new_v_s,
            new_lf_s, hg_p[None], hg_s[None], new_wk_p, new_wv_p, new_wk_s, new_wv_s)
```

```python
import functools

import numpy as np
import jax
import jax.numpy as jnp
from jax import lax
from jax.experimental import pallas as pl
from jax.experimental.pallas import tpu as pltpu

F32 = jnp.float32
BF16 = jnp.bfloat16

D_MODEL = 1024
DEPTH = 2
N_META = 16
BLOCK = 128
PAD = BLOCK - N_META
D_A = D_MODEL // 2
POOL_WINDOWS = (2, 4, 8, 16)
GC = D_A // len(POOL_WINDOWS)
W_MAX = 16
POOL_BUF = W_MAX - 1
H_B = D_MODEL // 128
DH_B = 64
D_B = H_B * DH_B
H_C = D_MODEL // 256
DK_C = 128
DV_C = 128
HK_C = H_C * DK_C
D_C = H_C * DV_C
HGRN_CHUNK = 64
H_D = D_MODEL // 128
KV_D = H_D // 4
G_D = H_D // KV_D
DH_D = 64
D_D = H_D * DH_D
WINDOW = 128
ALPHA = (2 * DEPTH) ** 0.25
LN_EPS = 1e-5
NEG = -1e30

LANES = 128
SUBLANES = 8
VMEM_LIMIT = 48 * 1024 * 1024

FOX_PAGES_PER_STEP = 32
FOX_SLOTS = 2
FOX_KEY_VARIANTS = 8
SEQS_PER_STEP = 8

_NT = (((1,), (1,)), ((), ()))


def _silu(x):
    return x * (1.0 / (1.0 + jnp.exp(-x)))


def _split3(x):
    hi = x.astype(BF16).astype(F32)
    r = x - hi
    mid = r.astype(BF16).astype(F32)
    return hi, mid, r - mid


def _dot(a, b):
    return jnp.dot(a, b, preferred_element_type=F32)


def _dot_nt(a, b):
    return lax.dot_general(a, b, _NT, preferred_element_type=F32)


def _dot3(m, x):
    hi, mid, lo = _split3(x)
    return _dot(m, hi) + _dot(m, mid) + _dot(m, lo)


def _dot3_rhs(x, m):
    hi, mid, lo = _split3(x)
    return _dot(hi, m) + _dot(mid, m) + _dot(lo, m)


def _params(sem):
    return pltpu.CompilerParams(dimension_semantics=sem, vmem_limit_bytes=VMEM_LIMIT)


def _proj_even_kernel(x_ref, w_ref, wf_ref, bf_ref, u_ref, ga_ref, q_ref, k_ref, v_ref, gb_ref,
                      lf_ref, *, tm, first_valid):
    xb = x_ref[...].astype(BF16)
    for i, o_ref in enumerate((u_ref, ga_ref, q_ref, k_ref, v_ref, gb_ref)):
        o_ref[...] = _dot(xb, w_ref[:, i * 512:(i + 1) * 512])
    z = _dot(xb, wf_ref[...]) + bf_ref[...]
    lf = jnp.minimum(z, 0.0) - jnp.log1p(jnp.exp(-jnp.abs(z)))
    pos = pl.program_id(1) * tm + lax.broadcasted_iota(jnp.int32, lf.shape, 0)
    lane = lax.broadcasted_iota(jnp.int32, lf.shape, 1)
    lf_ref[...] = jnp.where((pos >= first_valid) & (lane < H_B), lf, 0.0)


def _proj_even(x, w6, wf, bf, *, nb, seq, tm, first_valid):
    nt = seq // tm
    rows = nb * seq
    row_spec = lambda c: pl.BlockSpec((tm, c), lambda b, j: (b * nt + j, 0))
    full = lambda a: pl.BlockSpec(a.shape, lambda b, j: (0,) * a.ndim)
    outs = [jax.ShapeDtypeStruct((rows, 512), F32)] * 6 + [jax.ShapeDtypeStruct((rows, LANES), F32)]
    return pl.pallas_call(
        functools.partial(_proj_even_kernel, tm=tm, first_valid=first_valid),
        grid=(nb, nt),
        in_specs=[row_spec(D_MODEL), full(w6), full(wf), full(bf)],
        out_specs=[row_spec(512)] * 6 + [row_spec(LANES)],
        out_shape=outs,
        compiler_params=_params(("parallel", "parallel")),
        name="proj_even",
    )(x, w6, wf, bf)


def _proj_odd_kernel(x_ref, w_ref, gam_ref, qc_ref, kc_ref, lf_ref, vc_ref, gc_ref, qd_ref, kd_ref,
                     vd_ref, gd_ref, *, tm, first_valid):
    xb = x_ref[...].astype(BF16)
    g0 = gam_ref[0:1, :]
    g1 = gam_ref[1:2, :]
    mx = jnp.maximum(g0, g1)
    e0 = jnp.exp(g0 - mx)
    e1 = jnp.exp(g1 - mx)
    p0 = e0 / (e0 + e1)
    p1 = e1 / (e0 + e1)
    lb = (p0 + p1) - p0
    pos = pl.program_id(1) * tm + lax.broadcasted_iota(jnp.int32, (tm, 1), 0)
    valid = pos >= first_valid
    qc_ref[...] = _dot(xb, w_ref[:, 0:512])
    fc = _dot(xb, w_ref[:, 512:1024])
    f = lb + (1.0 - lb) * (1.0 / (1.0 + jnp.exp(-fc)))
    kc_ref[...] = jnp.where(valid, 1.0 - f, 0.0)
    lf_ref[...] = jnp.where(valid, jnp.log(f), 0.0)
    vc_ref[...] = _dot(xb, w_ref[:, 1024:1536])
    gc_ref[...] = _dot(xb, w_ref[:, 1536:2048])
    qd_ref[...] = _dot(xb, w_ref[:, 2048:2560])
    kd_ref[...] = _dot(xb, w_ref[:, 2560:2688])
    vd_ref[...] = _dot(xb, w_ref[:, 2688:2816])
    gd_ref[...] = _dot(xb, w_ref[:, 2816:3328])


def _proj_odd(x, w, gam, *, nb, seq, tm, first_valid):
    nt = seq // tm
    rows = nb * seq
    row_spec = lambda c: pl.BlockSpec((tm, c), lambda b, j: (b * nt + j, 0))
    full = lambda a: pl.BlockSpec(a.shape, lambda b, j: (0,) * a.ndim)
    widths = (512, 512, 512, 512, 512, 512, 128, 128, 512)
    return pl.pallas_call(
        functools.partial(_proj_odd_kernel, tm=tm, first_valid=first_valid),
        grid=(nb, nt),
        in_specs=[row_spec(D_MODEL), full(w), full(gam)],
        out_specs=[row_spec(c) for c in widths],
        out_shape=[jax.ShapeDtypeStruct((rows, c), F32) for c in widths],
        compiler_params=_params(("parallel", "parallel")),
        name="proj_odd",
    )(x, w, gam)


def _out_ln_kernel(a_ref, b_ref, x_ref, wa_ref, wb_ref, g_ref, bb_ref, o_ref, *, tm, first_valid):
    acc = _dot(a_ref[...].astype(BF16), wa_ref[...]) + _dot(b_ref[...].astype(BF16), wb_ref[...])
    y = ALPHA * x_ref[...] + acc
    mu = jnp.mean(y, axis=-1, keepdims=True)
    yc = y - mu
    var = jnp.mean(yc * yc, axis=-1, keepdims=True)
    o = yc * lax.rsqrt(var + LN_EPS) * g_ref[...] + bb_ref[...]
    pos = pl.program_id(1) * tm + lax.broadcasted_iota(jnp.int32, (tm, 1), 0)
    o_ref[...] = jnp.where(pos >= first_valid, o, 0.0)


def _out_ln(a, b, x, wa, wb, g, bb, *, nb, seq, tm, first_valid):
    nt = seq // tm
    row_spec = lambda c: pl.BlockSpec((tm, c), lambda i, j: (i * nt + j, 0))
    full = lambda t: pl.BlockSpec(t.shape, lambda i, j: (0,) * t.ndim)
    return pl.pallas_call(
        functools.partial(_out_ln_kernel, tm=tm, first_valid=first_valid),
        grid=(nb, nt),
        in_specs=[row_spec(512), row_spec(512), row_spec(D_MODEL), full(wa), full(wb), full(g), full(bb)],
        out_specs=row_spec(D_MODEL),
        out_shape=jax.ShapeDtypeStruct(x.shape, F32),
        compiler_params=_params(("parallel", "parallel")),
        name="out_ln",
    )(a, b, x, wa, wb, g, bb)


def _pool_kernel(u_ref, g_ref, wp_ref, sc_ref, y_ref, buf_ref, *, tr, nsub, first_valid):
    j = pl.program_id(1)
    pos = lax.broadcasted_iota(jnp.int32, (tr, 1), 0)
    if nsub == 1:
        pos = pos + j * tr
    for q in range(nsub):
        rs = slice(q * tr, (q + 1) * tr)
        if nsub == 1:
            @pl.when(j == 0)
            def _():
                buf_ref[0:W_MAX, :] = jnp.zeros((W_MAX, D_A), F32)

            @pl.when(j > 0)
            def _():
                buf_ref[0:W_MAX, :] = buf_ref[tr:tr + W_MAX, :]
        else:
            buf_ref[0:W_MAX, :] = jnp.zeros((W_MAX, D_A), F32)
        uf = jnp.where(pos >= first_valid, u_ref[rs, :], 0.0)
        buf_ref[W_MAX:W_MAX + tr, :] = uf
        for g, w in enumerate(POOL_WINDOWS):
            cs = slice(g * GC, (g + 1) * GC)
            s = buf_ref[W_MAX:W_MAX + tr, cs]
            for i in range(1, w):
                s = s + buf_ref[W_MAX - i:W_MAX - i + tr, cs]
            cnt = jnp.clip(pos - first_valid + 1, 1, w).astype(F32)
            d = s / cnt - uf[:, cs]
            y = _dot(d, wp_ref[g]) * sc_ref[:, cs]
            y_ref[rs, cs] = (y * _silu(g_ref[rs, cs])).astype(y_ref.dtype)


def _pool(u, g, wp, sc, *, nb, seq, tr, first_valid, nsub=1, ydt=F32):
    nt = seq // tr
    assert nsub == 1 or (nt == 1 and nb % nsub == 0)
    nb = nb // nsub
    row_spec = pl.BlockSpec((nsub * tr, D_A), lambda b, j: (b * nt + j, 0))
    return pl.pallas_call(
        functools.partial(_pool_kernel, tr=tr, nsub=nsub, first_valid=first_valid),
        grid=(nb, nt),
        in_specs=[row_spec, row_spec,
                  pl.BlockSpec(wp.shape, lambda b, j: (0, 0, 0)),
                  pl.BlockSpec(sc.shape, lambda b, j: (0, 0))],
        out_specs=row_spec,
        out_shape=jax.ShapeDtypeStruct(u.shape, ydt),
        scratch_shapes=[pltpu.VMEM((tr + W_MAX, D_A), F32)],
        compiler_params=_params(("parallel", "arbitrary")),
        name="pool",
    )(u, g, wp, sc)


def _fcum_kernel(lf_ref, aq_ref, ak_ref, *, nt):
    t = lax.broadcasted_iota(jnp.int32, (BLOCK, BLOCK), 0)
    lane = lax.broadcasted_iota(jnp.int32, (BLOCK, BLOCK), 1)
    tri = jnp.where(t >= lane, 1.0, 0.0).astype(F32)
    carry = jnp.zeros((1, LANES), F32)
    for j in range(nt):
        rs = slice(j * BLOCK, (j + 1) * BLOCK)
        f = _dot3(tri, lf_ref[rs, :]) + carry
        carry = f[BLOCK - 1:BLOCK, :]
        fh, fm, fl = _split3(f)
        f3 = fh + pltpu.roll(fm, H_B, 1) + pltpu.roll(fl, 2 * H_B, 1)
        aq_ref[rs, :] = jnp.where((lane >= 3 * H_B) & (lane < 6 * H_B), 1.0, f3)
        ak_ref[rs, :] = jnp.where(lane < 3 * H_B, 1.0, -pltpu.roll(f3, 3 * H_B, 1))


def _fcum(lf, *, nb, seq):
    spec = pl.BlockSpec((seq, LANES), lambda b: (b, 0))
    sds = jax.ShapeDtypeStruct(lf.shape, F32)
    return pl.pallas_call(
        functools.partial(_fcum_kernel, nt=seq // BLOCK),
        grid=(nb,),
        in_specs=[spec],
        out_specs=[spec, spec],
        out_shape=[sds, sds],
        compiler_params=_params(("parallel",)),
        name="fcum",
    )(lf)


def _fox_prompt_kernel(q_ref, k_ref, v_ref, aq_ref, ak_ref, gb_ref, y_ref, *, first_valid, klens):
    i = pl.program_id(1)
    lane = lax.broadcasted_iota(jnp.int32, (BLOCK, LANES), 1)
    low = lane < DH_B
    row = lax.broadcasted_iota(jnp.int32, (2 * BLOCK, 1), 0)
    qpos = i * BLOCK + (row & (BLOCK - 1))

    def attend(lk):
        kpos = lax.broadcasted_iota(jnp.int32, (1, lk), 1)
        mask = (kpos <= qpos) & (kpos >= first_valid)
        aq = aq_ref[...]
        ak = ak_ref[0:lk, :]
        for j in range(H_B // 2):
            cs = slice(j * LANES, (j + 1) * LANES)
            qp = q_ref[:, cs] * (DH_B ** -0.5)
            a0 = jnp.where(((lane & (H_B - 1)) == 2 * j) & (lane < 6 * H_B), aq, 0.0)
            a1 = jnp.where(((lane & (H_B - 1)) == 2 * j + 1) & (lane < 6 * H_B), aq, 0.0)
            qs = jnp.concatenate([
                jnp.concatenate([jnp.where(low, qp, 0.0), a0], axis=1),
                jnp.concatenate([jnp.where(low, 0.0, qp), a1], axis=1)], axis=0)
            ks = jnp.concatenate([k_ref[0:lk, cs], ak], axis=1)
            s = jnp.where(mask, _dot_nt(qs, ks), NEG)
            m = jnp.max(s, axis=-1, keepdims=True)
            p = jnp.exp(s - m)
            l = jnp.sum(p, axis=-1, keepdims=True)
            r = _dot(p, v_ref[0:lk, cs]) / l
            o = jnp.where(low, r[:BLOCK], r[BLOCK:])
            y_ref[:, cs] = (o * _silu(gb_ref[:, cs])).astype(y_ref.dtype)

    lo = 0
    for hi in klens:
        @pl.when((i >= lo) & (i < hi))
        def _(hi=hi):
            attend(hi * BLOCK)
        lo = hi


def _fox_prompt(q, k, v, aq, ak, gb, *, nb, seq, first_valid, ydt=F32):
    nt = seq // BLOCK
    nvar = min(FOX_KEY_VARIANTS, nt)
    klens = tuple(-(-nt * (x + 1) // nvar) for x in range(nvar))
    blk = lambda c: pl.BlockSpec((BLOCK, c), lambda b, i: (b * nt + i, 0))
    whole = lambda c: pl.BlockSpec((seq, c), lambda b, i: (b, 0))
    return pl.pallas_call(
        functools.partial(_fox_prompt_kernel, first_valid=first_valid, klens=klens),
        grid=(nb, nt),
        in_specs=[blk(D_B), whole(D_B), whole(D_B), blk(LANES), whole(LANES), blk(D_B)],
        out_specs=blk(D_B),
        out_shape=jax.ShapeDtypeStruct(q.shape, ydt),
        compiler_params=_params(("parallel", "parallel")),
        name="fox_prompt",
    )(q, k, v, aq, ak, gb)


def _fox_page_copies(pt_ref, kt_hbm, vt_hbm, lf_hbm, kbuf, vbuf, lbuf, sem, seq, step, slot, *, gp, npages):
    copies = []
    for g in range(gp):
        page = pt_ref[seq, npages - (step + 1) * gp + g]
        copies.append(pltpu.make_async_copy(kt_hbm.at[0, page], kbuf.at[slot, g], sem.at[slot]))
        copies.append(pltpu.make_async_copy(vt_hbm.at[0, page], vbuf.at[slot, g], sem.at[slot]))
        copies.append(pltpu.make_async_copy(lf_hbm.at[0, page], lbuf.at[slot, g], sem.at[slot]))
    return copies


def _fox_sample_kernel(pt_ref, *refs, gp, npages, side):
    q_ref, kn_ref, vn_ref, lft_ref, gb_ref, ut_ref, kt_hbm, vt_hbm, lf_hbm = refs[:9]
    n_in = 9 + (14 if side else 0)
    n_out = 1 + (3 if side else 0)
    y_ref = refs[n_in]
    qbd_ref, m_ref, l_ref, acc_ref, suf_ref, kbuf, vbuf, lbuf, sem = refs[n_in + n_out:n_in + n_out + 9]
    seq = pl.program_id(0)
    step = pl.program_id(1)
    nseq = pl.num_programs(0)
    steps = npages // gp
    lin = seq * steps + step
    copies = functools.partial(_fox_page_copies, pt_ref, kt_hbm, vt_hbm, lf_hbm, kbuf, vbuf, lbuf, sem,
                               gp=gp, npages=npages)

    def start(cs):
        for idx, c in enumerate(cs):
            c.start(priority=idx % 2)

    @pl.when(lin == 0)
    def _():
        for d in range(FOX_SLOTS - 1):
            start(copies(0, d, d))

    ahead = step + (FOX_SLOTS - 1)
    wrap = (ahead >= steps).astype(jnp.int32)
    seq_a = seq + wrap
    step_a = ahead - wrap * steps

    @pl.when(seq_a < nseq)
    def _():
        start(copies(seq_a, step_a, lax.rem(lin + (FOX_SLOTS - 1), FOX_SLOTS)))

    if side:
        hg_in, sw_in = refs[9:16], refs[16:23]
        y_hg, s_hg, y_sw = refs[n_in + 1:n_in + 4]
        st_hg = refs[n_in + n_out + 9]
        hu = lax.div(lin, side["hg_every"])

        @pl.when((lax.rem(lin, side["hg_every"]) == 0) & (hu < side["hg_units"]))
        def _():
            j = lax.rem(hu, side["nc"])
            _hgrn_step(*hg_in, None, y_hg, s_hg, st_hg, j == 0, j == side["nc"] - 1, c=HGRN_CHUNK, nsub=1)

        su = lax.div(lin, side["sw_every"])

        @pl.when((lax.rem(lin, side["sw_every"]) == 0) & (su < side["sw_units"]))
        def _():
            _swa_prompt_step(*sw_in, y_sw, lax.rem(su, side["nt"]), first_valid=side["first_valid"])

    slot = lax.rem(lin, FOX_SLOTS)
    for c in copies(seq, step, slot):
        c.wait()

    lane512 = lax.broadcasted_iota(jnp.int32, (SUBLANES, D_B), 1)
    nrow = H_B * SUBLANES

    @pl.when(step == 0)
    def _():
        qs = q_ref[...] * (DH_B ** -0.5)
        for h in range(H_B):
            qbd_ref[h * 8:(h + 1) * 8, :] = jnp.where((lane512 >> 6) == h, qs, 0.0)
        lft = lft_ref[0]
        lane = lax.broadcasted_iota(jnp.int32, (H_B, LANES), 1)
        gcum = jnp.zeros((H_B, LANES), F32)
        for u in range(SUBLANES):
            gcum = gcum + jnp.where(lane >= u, jnp.broadcast_to(lft[:, u:u + 1], (H_B, LANES)), 0.0)
        kpad = jnp.concatenate([kn_ref[...], jnp.zeros((LANES - 8, D_B), F32)], axis=0)
        vpad = jnp.concatenate([vn_ref[...], jnp.zeros((LANES - 8, D_B), F32)], axis=0)
        s = _dot_nt(qbd_ref[...], kpad)
        bias = jnp.concatenate(
            [jnp.broadcast_to(gcum[h:h + 1, :], (8, LANES)) for h in range(H_B)], axis=0)
        t = lax.broadcasted_iota(jnp.int32, (nrow, LANES), 0) & 7
        sl = lax.broadcasted_iota(jnp.int32, (nrow, LANES), 1)
        s = jnp.where(sl <= t, s - bias, NEG)
        m = jnp.max(s, axis=-1, keepdims=True)
        p = jnp.exp(s - m)
        m_ref[...] = m
        l_ref[...] = jnp.sum(p, axis=-1, keepdims=True)
        acc_ref[...] = _dot(p, vpad)
        suf_ref[...] = jnp.zeros(suf_ref.shape, F32)

    lfs = jnp.concatenate([lbuf[slot, g] for g in range(gp)], axis=0)
    wt = _dot3_rhs(lfs, ut_ref[...])
    suf = suf_ref[...]
    biases = [None] * gp
    for g in range(gp - 1, -1, -1):
        biases[g] = wt[g * 8:(g + 1) * 8, :LANES] + suf
        suf = suf + wt[g * 8:(g + 1) * 8, LANES:]
    suf_ref[...] = suf

    qbd = qbd_ref[...]
    ss = []
    for g in range(gp):
        sg = _dot(qbd, kbuf[slot, g])
        bg = jnp.concatenate(
            [jnp.broadcast_to(biases[g][h:h + 1, :], (8, LANES)) for h in range(H_B)], axis=0)
        ss.append(sg + bg)
    m_prev = m_ref[...]
    m_cur = ss[0]
    for g in range(1, gp):
        m_cur = jnp.maximum(m_cur, ss[g])
    m_new = jnp.maximum(m_prev, jnp.max(m_cur, axis=-1, keepdims=True))
    alpha = jnp.exp(m_prev - m_new)
    psum = None
    pv = None
    for g in range(gp):
        p = jnp.exp(ss[g] - m_new)
        psum = p if psum is None else psum + p
        d = _dot_nt(p, vbuf[slot, g])
        pv = d if pv is None else pv + d
    m_ref[...] = m_new
    l_ref[...] = alpha * l_ref[...] + jnp.sum(psum, axis=-1, keepdims=True)
    acc_ref[...] = alpha * acc_ref[...] + pv

    @pl.when(step == steps - 1)
    def _():
        o = acc_ref[...] / l_ref[...]
        out = jnp.zeros((SUBLANES, D_B), F32)
        for h in range(H_B):
            out = out + jnp.where((lane512 >> 6) == h, o[h * 8:(h + 1) * 8, :], 0.0)
        y_ref[...] = out * _silu(gb_ref[...])


def _fox_sample(page_table, q, kn, vn, lft, gb, kt, vt, lfc, *, gp, hgrn_args=None, swa_args=None,
                nb=0, seq=0, first_valid=0, ydt=F32):
    n, npages = page_table.shape
    steps = npages // gp
    total = n * steps
    assert npages % gp == 0 and steps >= FOX_SLOTS - 1
    s_i = np.arange(LANES)
    ut = np.concatenate([(s_i[:, None] > s_i[None, :]).astype(np.float32),
                         np.ones((LANES, LANES), np.float32)], axis=1)
    row = lambda c: pl.BlockSpec((8, c), lambda i, s, pt: (i, 0))
    hbm = pl.BlockSpec(memory_space=pl.ANY)
    const = lambda a: pl.BlockSpec(a.shape, lambda i, s, pt: (0,) * a.ndim)
    in_specs = [row(D_B), row(D_B), row(D_B),
                pl.BlockSpec((1, H_B, LANES), lambda i, s, pt: (i, 0, 0)),
                row(D_B), const(ut), hbm, hbm, hbm]
    args = [q, kn, vn, lft, gb, jnp.asarray(ut), kt, vt, lfc]
    out_specs = [pl.BlockSpec((8, D_B), lambda i, s, pt: (i, 0))]
    out_shape = [jax.ShapeDtypeStruct(q.shape, F32)]
    nrow = H_B * SUBLANES
    scratch = [pltpu.VMEM((nrow, D_B), F32), pltpu.VMEM((nrow, 1), F32),
               pltpu.VMEM((nrow, 1), F32), pltpu.VMEM((nrow, D_B), F32),
               pltpu.VMEM((H_B, LANES), F32),
               pltpu.VMEM((FOX_SLOTS, gp, D_B, LANES), F32),
               pltpu.VMEM((FOX_SLOTS, gp, D_B, LANES), F32),
               pltpu.VMEM((FOX_SLOTS, gp, H_B, LANES), F32),
               pltpu.SemaphoreType.DMA((FOX_SLOTS,))]
    side = None
    if hgrn_args is not None:
        nc, nt = seq // HGRN_CHUNK, seq // BLOCK
        hg_units, sw_units = nb * nc, nb * nt
        assert total >= hg_units and total >= sw_units
        side = dict(hg_every=total // hg_units, hg_units=hg_units, nc=nc,
                    sw_every=total // sw_units, sw_units=sw_units, nt=nt, first_valid=first_valid)

        def unit(every, units):
            return lambda i, s, pt: jnp.minimum(lax.div(i * steps + s, every), units - 1)

        hu, su = unit(side["hg_every"], hg_units), unit(side["sw_every"], sw_units)
        hq, hk, hlf, hv, hg, hng = hgrn_args
        ssk, sq, sk_, sv, sg = swa_args
        ms = jnp.asarray(_hgrn_mstack(HGRN_CHUNK))
        hblk = pl.BlockSpec((HGRN_CHUNK, HK_C), lambda i, s, pt: (hu(i, s, pt), 0))
        st_spec = pl.BlockSpec((1, H_C, DK_C, DV_C), lambda i, s, pt: (lax.div(hu(i, s, pt), nc), 0, 0, 0))
        cur = lambda c: pl.BlockSpec((BLOCK, c), lambda i, s, pt: (su(i, s, pt), 0))

        def prev_idx(i, s, pt):
            u = su(i, s, pt)
            return u - jnp.minimum(lax.rem(u, nt), 1)

        prev = lambda c: pl.BlockSpec((BLOCK, c), lambda i, s, pt: (prev_idx(i, s, pt), 0))
        kvw = KV_D * DH_D
        in_specs += [hblk] * 5 + [const(ms), const(hng)]
        in_specs += [pl.BlockSpec(memory_space=pltpu.SMEM), cur(D_D), prev(kvw), cur(kvw), prev(kvw),
                     cur(kvw), cur(D_D)]
        args += [hq, hk, hlf, hv, hg, ms, hng, ssk, sq, sk_, sk_, sv, sv, sg]
        out_specs += [hblk, st_spec, cur(D_D)]
        out_shape += [jax.ShapeDtypeStruct(hq.shape, ydt),
                      jax.ShapeDtypeStruct((nb, H_C, DK_C, DV_C), F32),
                      jax.ShapeDtypeStruct(sq.shape, ydt)]
        scratch.append(pltpu.VMEM((H_C, DV_C, DK_C), F32))
    grid_spec = pltpu.PrefetchScalarGridSpec(
        num_scalar_prefetch=1, grid=(n, steps), in_specs=in_specs,
        out_specs=out_specs if side else out_specs[0], scratch_shapes=scratch)
    return pl.pallas_call(
        functools.partial(_fox_sample_kernel, gp=gp, npages=npages, side=side),
        grid_spec=grid_spec,
        out_shape=out_shape if side else out_shape[0],
        compiler_params=_params(("arbitrary", "arbitrary")),
        name="fox_sample",
    )(page_table, *args)


def _hgrn_levels(c):
    out = []
    m = c // 2
    while m >= 1:
        out.append(m)
        m //= 2
    return tuple(out)


def _hgrn_mstack(c):
    t = np.arange(c)[:, None]
    u = np.arange(c)[None, :]
    mats = [(u <= t)]
    for m in _hgrn_levels(c):
        r = (t // (2 * m)) * (2 * m) + m - 1
        second = (t % (2 * m)) >= m
        mats.append(np.where(second, (u > r) & (u <= t), (u > t) & (u <= r)))
    return np.concatenate(mats, axis=0).astype(np.float32)


def _hgrn_kernel(*refs, c, nsub, has_s0):
    if has_s0:
        q_ref, k_ref, lf_ref, v_ref, g_ref, m_ref, ng_ref, s0_ref, y_ref, so_ref, st_ref = refs
    else:
        q_ref, k_ref, lf_ref, v_ref, g_ref, m_ref, ng_ref, y_ref, so_ref, st_ref = refs
        s0_ref = None
    ci = pl.program_id(1)
    _hgrn_step(q_ref, k_ref, lf_ref, v_ref, g_ref, m_ref, ng_ref, s0_ref, y_ref, so_ref, st_ref,
               ci == 0, ci == pl.num_programs(1) - 1, c=c, nsub=nsub)


def _hgrn_step(q_ref, k_ref, lf_ref, v_ref, g_ref, m_ref, ng_ref, s0_ref, y_ref, so_ref, st_ref,
               first, last, *, c, nsub):
    has_s0 = s0_ref is not None
    levels = _hgrn_levels(c)
    units = [(q, h) for q in range(nsub) for h in range(H_C)]

    @pl.when(first)
    def _():
        for q, h in units:
            st_ref[q * H_C + h] = s0_ref[q, h].T if has_s0 else jnp.zeros((DV_C, DK_C), F32)

    t_col = lax.broadcasted_iota(jnp.int32, (c, 1), 0)
    t_row = lax.broadcasted_iota(jnp.int32, (c, c), 0)
    s_col = lax.broadcasted_iota(jnp.int32, (c, c), 1)
    rpad = max(LANES - c, 0)
    mstack = m_ref[...]
    sub = {}
    for q in range(nsub):
        rs = slice(q * c, (q + 1) * c)
        e = _dot3(mstack, lf_ref[rs, :])
        b = e[0:c]
        qv, kv = q_ref[rs, :], k_ref[rs, :]
        bl = b[c - 1:c, :]
        qx, kx = [], []
        for li, m in enumerate(levels):
            x = jnp.exp(e[(li + 1) * c:(li + 2) * c])
            second = (t_col & (2 * m - 1)) >= m
            qx.append(jnp.where(second, qv * x, 0.0))
            kx.append(jnp.where(second, 0.0, kv * x))
        sub[q] = dict(rs=rs, qb=qv * jnp.exp(b), qk=qv * kv, kp=kv * jnp.exp(bl - b), dec=jnp.exp(bl),
                      qx=qx, kx=kx, v=v_ref[rs, :])
    o_inter, a_mat = {}, {}
    for q, h in units:
        cs = slice(h * DK_C, (h + 1) * DK_C)
        d = sub[q]
        o_inter[q, h] = _dot_nt(d["qb"][:, cs], st_ref[q * H_C + h])
        a = jnp.where(t_row == s_col, jnp.sum(d["qk"][:, cs], axis=-1, keepdims=True), 0.0)
        for li, m in enumerate(levels):
            sh = (2 * m).bit_length() - 1
            am = _dot_nt(d["qx"][li][:, cs], d["kx"][li][:, cs])
            a = a + jnp.where((t_row >> sh) == (s_col >> sh), am, 0.0)
        a_mat[q, h] = a
    for q, h in units:
        cs = slice(h * DK_C, (h + 1) * DK_C)
        d = sub[q]
        v = d["v"][:, cs]
        kp = d["kp"][:, cs]
        o = o_inter[q, h] + _dot(a_mat[q, h], v)
        if rpad:
            v = jnp.concatenate([v, jnp.zeros((rpad, DV_C), F32)], axis=0)
            kp = jnp.concatenate([kp, jnp.zeros((rpad, DK_C), F32)], axis=0)
        st_ref[q * H_C + h] = st_ref[q * H_C + h] * d["dec"][:, cs] + _dot(v.T, kp)
        y = o * lax.rsqrt(jnp.mean(o * o, axis=-1, keepdims=True) + LN_EPS) * ng_ref[...]
        y_ref[d["rs"], cs] = (y * _silu(g_ref[d["rs"], cs])).astype(y_ref.dtype)

    @pl.when(last)
    def _():
        for q, h in units:
            so_ref[q, h] = st_ref[q * H_C + h].T


def _hgrn(q, k, lf, v, g, ng, s0, *, nb, seq, c, nsub=1, ydt=F32):
    nc = seq // c
    assert nsub == 1 or (nc == 1 and nb % nsub == 0)
    ms = jnp.asarray(_hgrn_mstack(c))
    blk = pl.BlockSpec((nsub * c, HK_C), lambda b, j: (b * nc + j, 0))
    st_spec = pl.BlockSpec((nsub, H_C, DK_C, DV_C), lambda b, j: (b, 0, 0, 0))
    in_specs = [blk, blk, blk, blk, blk,
                pl.BlockSpec(ms.shape, lambda b, j: (0, 0)),
                pl.BlockSpec(ng.shape, lambda b, j: (0, 0))]
    args = [q, k, lf, v, g, ms, ng]
    if s0 is not None:
        in_specs.append(st_spec)
        args.append(s0)
    return pl.pallas_call(
        functools.partial(_hgrn_kernel, c=c, nsub=nsub, has_s0=s0 is not None),
        grid=(nb // nsub, nc),
        in_specs=in_specs,
        out_specs=[blk, st_spec],
        out_shape=[jax.ShapeDtypeStruct(q.shape, ydt),
                   jax.ShapeDtypeStruct((nb, H_C, DK_C, DV_C), F32)],
        scratch_shapes=[pltpu.VMEM((nsub * H_C, DV_C, DK_C), F32)],
        compiler_params=_params(("parallel", "arbitrary")),
        name="hgrn",
    )(*args)


def _swa_prompt_kernel(sk_ref, q_ref, kp_ref, kc_ref, vp_ref, vc_ref, g_ref, y_ref, *, first_valid):
    _swa_prompt_step(sk_ref, q_ref, kp_ref, kc_ref, vp_ref, vc_ref, g_ref, y_ref, pl.program_id(1),
                     first_valid=first_valid)


def _swa_prompt_step(sk_ref, q_ref, kp_ref, kc_ref, vp_ref, vc_ref, g_ref, y_ref, i, *, first_valid):
    lane = lax.broadcasted_iota(jnp.int32, (BLOCK, LANES), 1)
    low = lane < DH_D
    kk = jnp.concatenate([kp_ref[...], kc_ref[...]], axis=0)
    vv = jnp.concatenate([vp_ref[...], vc_ref[...]], axis=0)
    kk_sw = pltpu.roll(kk, DH_D, 1)
    vv_sw = pltpu.roll(vv, DH_D, 1)
    kj = lax.broadcasted_iota(jnp.int32, (BLOCK, 2 * BLOCK), 1)
    tq = lax.broadcasted_iota(jnp.int32, (BLOCK, 2 * BLOCK), 0)
    dist = tq + BLOCK - kj
    kpos = i * BLOCK - BLOCK + kj
    mask = (dist >= 0) & (dist <= WINDOW) & (kpos >= first_valid)
    distf = dist.astype(F32)
    ss = []
    for h in range(H_D):
        j, e, kvh = h // 2, h % 2, h // G_D
        qp = q_ref[:, j * LANES:(j + 1) * LANES] * (DH_D ** -0.5)
        qm = jnp.where(low, qp, 0.0) if e == 0 else jnp.where(low, 0.0, qp)
        s = _dot_nt(qm, kk if e == kvh else kk_sw) - (2.0 ** -(h + 1)) * distf
        ss.append(jnp.where(mask, s, NEG))
    ps, dens = [], []
    for h in range(H_D):
        sink = sk_ref[h]
        m = jnp.maximum(jnp.max(ss[h], axis=-1, keepdims=True), sink)
        p = jnp.exp(ss[h] - m)
        ps.append(p)
        dens.append(jnp.sum(p, axis=-1, keepdims=True) + jnp.exp(sink - m))
    outs = [_dot(ps[h], vv if h % 2 == h // G_D else vv_sw) / dens[h] for h in range(H_D)]
    for j in range(H_D // 2):
        cs = slice(j * LANES, (j + 1) * LANES)
        y_ref[:, cs] = (jnp.where(low, outs[2 * j], outs[2 * j + 1]) * _silu(g_ref[:, cs])).astype(y_ref.dtype)


def _swa_prompt(sinks, q, k, v, g, *, nb, seq, first_valid, ydt=F32):
    nt = seq // BLOCK
    cur = lambda c: pl.BlockSpec((BLOCK, c), lambda b, i: (b * nt + i, 0))
    prev = lambda c: pl.BlockSpec((BLOCK, c), lambda b, i: (b * nt + jnp.maximum(i - 1, 0), 0))
    kvw = KV_D * DH_D
    return pl.pallas_call(
        functools.partial(_swa_prompt_kernel, first_valid=first_valid),
        grid=(nb, nt),
        in_specs=[pl.BlockSpec(memory_space=pltpu.SMEM),
                  cur(D_D), prev(kvw), cur(kvw), prev(kvw), cur(kvw), cur(D_D)],
        out_specs=cur(D_D),
        out_shape=jax.ShapeDtypeStruct(q.shape, ydt),
        compiler_params=_params(("parallel", "parallel")),
        name="swa_prompt",
    )(sinks, q, k, k, v, v, g)


def _swa_sample_kernel(sk_ref, q_ref, kn_ref, vn_ref, kt_ref, vt_ref, g_ref, y_ref, *, nsub):
    nrow = H_D * SUBLANES
    lane8 = lax.broadcasted_iota(jnp.int32, (SUBLANES, LANES), 1)
    rows = lax.broadcasted_iota(jnp.int32, (nrow, 1), 0)
    hrow = rows >> 3
    t = rows & 7
    slope = jnp.zeros((nrow, 1), F32)
    sink = jnp.zeros((nrow, 1), F32)
    for h in range(H_D):
        slope = jnp.where(hrow == h, 2.0 ** -(h + 1), slope)
        sink = jnp.where(hrow == h, sk_ref[h], sink)
    kj = lax.broadcasted_iota(jnp.int32, (nrow, LANES), 1)
    d_old = t + WINDOW - kj
    d_new = t - kj
    for u in range(nsub):
        rs = slice(u * 8, (u + 1) * 8)
        blocks = []
        for h in range(H_D):
            j, e, kvh = h // 2, h % 2, h // G_D
            qp = q_ref[rs, j * LANES:(j + 1) * LANES] * (DH_D ** -0.5)
            src = qp if e == kvh else pltpu.roll(qp, DH_D, 1)
            blocks.append(jnp.where((lane8 >> 6) == kvh, src, 0.0))
        qbd = jnp.concatenate(blocks, axis=0)
        kpad = jnp.concatenate([kn_ref[rs, :], jnp.zeros((LANES - 8, LANES), F32)], axis=0)
        vpad = jnp.concatenate([vn_ref[rs, :], jnp.zeros((LANES - 8, LANES), F32)], axis=0)
        s_old = _dot(qbd, kt_ref[0, u]) - slope * d_old.astype(F32)
        s_old = jnp.where(d_old <= WINDOW, s_old, NEG)
        s_new = _dot_nt(qbd, kpad) - slope * d_new.astype(F32)
        s_new = jnp.where(d_new >= 0, s_new, NEG)
        m = jnp.maximum(jnp.maximum(jnp.max(s_old, axis=-1, keepdims=True),
                                    jnp.max(s_new, axis=-1, keepdims=True)), sink)
        p_old = jnp.exp(s_old - m)
        p_new = jnp.exp(s_new - m)
        den = (jnp.sum(p_old, axis=-1, keepdims=True) + jnp.sum(p_new, axis=-1, keepdims=True)
               + jnp.exp(sink - m))
        o = (_dot_nt(p_old, vt_ref[0, u]) + _dot(p_new, vpad)) / den
        o_sw = pltpu.roll(o, DH_D, 1)
        for j in range(H_D // 2):
            parts = []
            for e in range(2):
                h = 2 * j + e
                parts.append((o if e == h // G_D else o_sw)[h * 8:(h + 1) * 8, :])
            cs = slice(j * LANES, (j + 1) * LANES)
            y_ref[rs, cs] = jnp.where(lane8 < DH_D, parts[0], parts[1]) * _silu(g_ref[rs, cs])


def _swa_sample(sinks, q, kn, vn, kt, vt, g, *, nsub):
    n = q.shape[0] // 8
    assert n % nsub == 0
    row = lambda c: pl.BlockSpec((nsub * 8, c), lambda i: (i, 0))
    buf = pl.BlockSpec((1, nsub, LANES, LANES), lambda i: (0, i, 0, 0))
    kvw = KV_D * DH_D
    return pl.pallas_call(
        functools.partial(_swa_sample_kernel, nsub=nsub),
        grid=(n // nsub,),
        in_specs=[pl.BlockSpec(memory_space=pltpu.SMEM), row(D_D), row(kvw), row(kvw), buf, buf, row(D_D)],
        out_specs=row(D_D),
        out_shape=jax.ShapeDtypeStruct(q.shape, F32),
        compiler_params=_params(("parallel",)),
        name="swa_sample",
    )(sinks, q, kn, vn, kt, vt, g)


def kernel(x_prompt, x_sample, cache_pool, cache_k, cache_v, cache_logf, state_hgrn, cache_win_k,
           cache_win_v, page_table, meta_tokens, w_in_e, b_f, w_pool, pool_scale, w_out_e, ln_g_e,
           ln_b_e, w_in_o, hgrn_gamma, hgrn_norm_g, sinks, w_out_o, ln_g_o, ln_b_o):
    assert w_in_e.shape[0] == 1 and w_in_o.shape[0] == 1 and hgrn_gamma.shape[0] == DEPTH
    nb, sp, _ = x_prompt.shape
    ns, ts, _ = x_sample.shape
    assert ts == 8 and sp % BLOCK == 0
    lp = sp + BLOCK
    n_pool = cache_k.shape[1]
    wb = cache_win_k.shape[2]
    assert wb == WINDOW and cache_k.shape[2] == LANES
    ptm = lp // 4
    nsub = max(d for d in range(1, SEQS_PER_STEP + 1) if ns % d == 0)
    stm = min(256, ns * ts)
    assert lp % (4 * SUBLANES) == 0 and (ns * ts) % stm == 0

    hp = jnp.concatenate([jnp.zeros((nb, PAD, D_MODEL), F32),
                          jnp.broadcast_to(meta_tokens, (nb, N_META, D_MODEL)), x_prompt], axis=1)
    hp = hp.reshape(nb * lp, D_MODEL)
    hs = x_sample.reshape(ns * ts, D_MODEL)

    we = w_in_e[0]
    w6 = we[:, :6 * 512].astype(BF16)
    wf = jnp.pad(we[:, 6 * 512:], ((0, 0), (0, LANES - H_B))).astype(BF16)
    bf = jnp.pad(b_f[0][None, :], ((0, 0), (0, LANES - H_B)))
    wa = w_out_e[0, :D_A].astype(BF16)
    wbm = w_out_e[0, D_A:].astype(BF16)
    sc = pool_scale[0][None, :]
    lng, lnb = ln_g_e[0][None, :], ln_b_e[0][None, :]

    u_p, ga_p, q_p, k_p, v_p, gb_p, lf_p = _proj_even(hp, w6, wf, bf, nb=nb, seq=lp, tm=ptm, first_valid=PAD)
    ya_p = _pool(u_p, ga_p, w_pool[0], sc, nb=nb, seq=lp, tr=ptm, first_valid=PAD, ydt=BF16)
    aq, ak = _fcum(lf_p, nb=nb, seq=lp)
    yb_p = _fox_prompt(q_p, k_p, v_p, aq, ak, gb_p, nb=nb, seq=lp, first_valid=PAD, ydt=BF16)
    hp1 = _out_ln(ya_p, yb_p, hp, wa, wbm, lng, lnb, nb=nb, seq=lp, tm=ptm, first_valid=PAD)

    wo = w_in_o[0].astype(BF16)
    ng = hgrn_norm_g[0][None, :]
    sk = sinks[0]
    qc, kc, lfh, vc, gc, qd, kd, vd, gd = _proj_odd(hp1, wo, hgrn_gamma, nb=nb, seq=lp, tm=ptm, first_valid=PAD)

    u_s, ga_s, q_s, k_s, v_s, gb_s, lf_s = _proj_even(hs, w6, wf, bf, nb=1, seq=ns * ts, tm=stm, first_valid=0)
    hist = POOL_BUF + ts + 1
    u3 = u_s.reshape(ns, ts, D_A)
    u_ext = jnp.concatenate([cache_pool[0], u3, jnp.zeros((ns, 1, D_A), F32)], axis=1)
    g_ext = jnp.concatenate([jnp.zeros((ns, POOL_BUF, D_A), F32), ga_s.reshape(ns, ts, D_A),
                             jnp.zeros((ns, 1, D_A), F32)], axis=1)
    ya_s = _pool(u_ext.reshape(ns * hist, D_A), g_ext.reshape(ns * hist, D_A), w_pool[0], sc,
                 nb=ns, seq=hist, tr=hist, first_valid=0, nsub=nsub)
    ya_s = ya_s.reshape(ns, hist, D_A)[:, POOL_BUF:POOL_BUF + ts].reshape(ns * ts, D_A)
    kt = jnp.transpose(cache_k, (0, 1, 3, 4, 2)).reshape(1, n_pool, D_B, LANES)
    vt = jnp.transpose(cache_v, (0, 1, 3, 4, 2)).reshape(1, n_pool, D_B, LANES)
    lfc = jnp.transpose(cache_logf, (0, 1, 3, 2))
    lft = jnp.transpose(lf_s[:, :H_B].reshape(ns, ts, H_B), (0, 2, 1))
    lft = jnp.pad(lft, ((0, 0), (0, 0), (0, LANES - ts)))
    fox_steps = ns * (page_table.shape[1] // FOX_PAGES_PER_STEP)
    ride = fox_steps >= nb * (lp // HGRN_CHUNK)
    if ride:
        yb_s, yc_p, hg_p, yd_p = _fox_sample(
            page_table, q_s, k_s, v_s, lft, gb_s, kt, vt, lfc, gp=FOX_PAGES_PER_STEP,
            hgrn_args=(qc, kc, lfh, vc, gc, ng), swa_args=(sk, qd, kd, vd, gd),
            nb=nb, seq=lp, first_valid=PAD, ydt=BF16)
    else:
        yb_s = _fox_sample(page_table, q_s, k_s, v_s, lft, gb_s, kt, vt, lfc, gp=FOX_PAGES_PER_STEP)
        yc_p, hg_p = _hgrn(qc, kc, lfh, vc, gc, ng, None, nb=nb, seq=lp, c=HGRN_CHUNK, ydt=BF16)
        yd_p = _swa_prompt(sk, qd, kd, vd, gd, nb=nb, seq=lp, first_valid=PAD, ydt=BF16)
    hs1 = _out_ln(ya_s, yb_s, hs, wa, wbm, lng, lnb, nb=1, seq=ns * ts, tm=stm, first_valid=0)

    new_pool_p = u_p.reshape(nb, lp, D_A)[:, -POOL_BUF:][None]
    new_pool_s = u_ext[:, ts:ts + POOL_BUF][None]
    new_k_p = k_p.reshape(nb, lp, H_B, DH_B)[:, PAD:][None]
    new_v_p = v_p.reshape(nb, lp, H_B, DH_B)[:, PAD:][None]
    new_lf_p = lf_p.reshape(nb, lp, LANES)[:, PAD:, :H_B][None]
    new_k_s = k_s.reshape(ns, ts, H_B, DH_B)[None]
    new_v_s = v_s.reshape(ns, ts, H_B, DH_B)[None]
    new_lf_s = lf_s[:, :H_B].reshape(ns, ts, H_B)[None]

    wc = w_out_o[0, :D_C].astype(BF16)
    wd = w_out_o[0, D_C:].astype(BF16)
    lng, lnb = ln_g_o[0][None, :], ln_b_o[0][None, :]

    hp2 = _out_ln(yc_p, yd_p, hp1, wc, wd, lng, lnb, nb=nb, seq=lp, tm=ptm, first_valid=PAD)

    qc, kc, lfh, vc, gc, qd, kd_s, vd_s, gd = _proj_odd(hs1, wo, hgrn_gamma, nb=1, seq=ns * ts, tm=stm, first_valid=0)
    yc_s, hg_s = _hgrn(qc, kc, lfh, vc, gc, ng, state_hgrn[0], nb=ns, seq=ts, c=ts, nsub=nsub)
    kvw = KV_D * DH_D
    wkt = jnp.transpose(cache_win_k, (0, 1, 3, 4, 2)).reshape(1, ns, kvw, wb)
    wvt = jnp.transpose(cache_win_v, (0, 1, 3, 4, 2)).reshape(1, ns, kvw, wb)
    yd_s = _swa_sample(sk, qd, kd_s, vd_s, wkt, wvt, gd, nsub=nsub)
    hs2 = _out_ln(yc_s, yd_s, hs1, wc, wd, lng, lnb, nb=1, seq=ns * ts, tm=stm, first_valid=0)

    kd4 = kd.reshape(nb, lp, KV_D, DH_D)
    vd4 = vd.reshape(nb, lp, KV_D, DH_D)
    new_wk_p = kd4[:, -wb:][None]
    new_wv_p = vd4[:, -wb:][None]
    new_wk_s = jnp.concatenate([cache_win_k[0], kd_s.reshape(ns, ts, KV_D, DH_D)], axis=1)[:, -wb:][None]
    new_wv_s = jnp.concatenate([cache_win_v[0], vd_s.reshape(ns, ts, KV_D, DH_D)], axis=1)[:, -wb:][None]

    y_prompt = hp2.reshape(nb, lp, D_MODEL)[:, BLOCK:]
    y_sample = hs2.reshape(ns, ts, D_MODEL)
    return (y_prompt, y_sample, new_pool_p, new_pool_s, new_k_p, new_v_p, new_lf_p, new_k_s, new_v_s,
            new_lf_s, hg_p[None], hg_s[None], new_wk_p, new_wv_p, new_wk_s, new_wv_s)
```
